```python
import math
import jax, jax.numpy as jnp
from jax import lax
import numpy as np

D_MODEL = 2048
BATCH = 4
SEQ = 2048
DEPTH = 1
DEC_BATCH = 128
DEC_SEQ = 1
PAST_LEN = 16384
PAGE_SIZE = 128

MLA_HEADS = 8
NOPE_DIM = 128
ROPE_DIM = 64
V_HEAD_DIM = 128
Q_RANK = 512
KV_RANK = 512
MLA_WIDTH = MLA_HEADS * V_HEAD_DIM
SOFTMAX_SCALE = 1.0 / math.sqrt(NOPE_DIM + ROPE_DIM)
ROPE_THETA = 10000.0
Q_BLOCK = 128
CM_GROUPS = 8
CM_HEAD = 128
CM_WIDTH = CM_GROUPS * CM_HEAD
CHUNK = 128
MIX_WIDTH = MLA_WIDTH + CM_WIDTH
IN_COLS = Q_RANK + KV_RANK + ROPE_DIM + 2 * CM_WIDTH
FFN_HIDDEN = -(-8 * D_MODEL // (3 * 256)) * 256
PLE_DIM = 256
EPS = 1e-6
NEG = -1e30

kernel_name = "hybrid_mla_chunkmlp_decoder_step"


def rmsnorm(x, g):
    xf = x.astype(jnp.float32)
    y = xf * lax.rsqrt(jnp.mean(xf * xf, axis=-1, keepdims=True) + EPS)
    return (y * g.astype(jnp.float32)).astype(x.dtype)


def rope(x, pos):
    half = ROPE_DIM // 2
    inv = ROPE_THETA ** (-jnp.arange(half, dtype=jnp.float32) / half)
    ang = pos.astype(jnp.float32)[:, None] * inv[None, :]
    ang = ang.reshape(ang.shape[0], *([1] * (x.ndim - 3)), half)
    cos, sin = jnp.cos(ang), jnp.sin(ang)
    x1 = x[..., :half].astype(jnp.float32)
    x2 = x[..., half:].astype(jnp.float32)
    return jnp.concatenate([x1 * cos - x2 * sin, x2 * cos + x1 * sin], axis=-1).astype(x.dtype)


def mixer_projections(h, pos, w_in, g_q, w_uq, g_kv, w_uk):
    z = h @ w_in
    c_q, c_kv, k_pe, z_cm = jnp.split(
        z, [Q_RANK, Q_RANK + KV_RANK, Q_RANK + KV_RANK + ROPE_DIM], axis=-1)
    q = jnp.einsum('bsr,rhd->bshd', rmsnorm(c_q, g_q), w_uq)
    q_nope, q_pe = q[..., :NOPE_DIM], q[..., NOPE_DIM:]
    q_pe = rope(q_pe, pos)
    q_lat = jnp.einsum('bshd,rhd->bshr', q_nope, w_uk)
    c_kv = rmsnorm(c_kv, g_kv)
    k_pe = rope(k_pe, pos)
    u, v = jnp.split(jax.nn.gelu(z_cm), 2, axis=-1)
    return q_lat, q_pe, c_kv, k_pe, u, v


def mla_prompt(q_lat, q_pe, c_kv, k_pe):
    B, S = q_lat.shape[0], q_lat.shape[1]
    nb = S // Q_BLOCK
    qlb = q_lat.reshape(B, nb, Q_BLOCK, MLA_HEADS, KV_RANK).transpose(1, 0, 2, 3, 4)
    qpb = q_pe.reshape(B, nb, Q_BLOCK, MLA_HEADS, ROPE_DIM).transpose(1, 0, 2, 3, 4)
    kpos = jnp.arange(S)

    def block(args):
        ql, qp, i = args
        s = (jnp.einsum('bqhr,bkr->bhqk', ql, c_kv)
             + jnp.einsum('bqhd,bkd->bhqk', qp, k_pe)).astype(jnp.float32) * SOFTMAX_SCALE
        qpos = i * Q_BLOCK + jnp.arange(Q_BLOCK)
        s = jnp.where(kpos[None, :] <= qpos[:, None], s, NEG)
        p = jax.nn.softmax(s, axis=-1)
        return jnp.einsum('bhqk,bkr->bqhr', p.astype(c_kv.dtype), c_kv)

    o = lax.map(block, (qlb, qpb, jnp.arange(nb)))
    return o.transpose(1, 0, 2, 3, 4).reshape(B, S, MLA_HEADS, KV_RANK)


def mla_sample(q_lat, q_pe, c_kv_new, k_pe_new, cache_ckv, cache_kpe, page_table):
    Bd, Sd = q_lat.shape[0], q_lat.shape[1]
    m0 = jnp.full((Bd, MLA_HEADS, Sd), NEG, jnp.float32)
    l0 = jnp.zeros((Bd, MLA_HEADS, Sd), jnp.float32)
    a0 = jnp.zeros((Bd, MLA_HEADS, Sd, KV_RANK), jnp.float32)

    def step(carry, pages):
        m, l, acc = carry
        ck = cache_ckv[pages]
        kp = cache_kpe[pages]
        s = (jnp.einsum('bqhr,bkr->bhqk', q_lat, ck)
             + jnp.einsum('bqhd,bkd->bhqk', q_pe, kp)).astype(jnp.float32) * SOFTMAX_SCALE
        m_new = jnp.maximum(m, s.max(-1))
        corr = jnp.exp(m - m_new)
        p = jnp.exp(s - m_new[..., None])
        l = l * corr + p.sum(-1)
        acc = acc * corr[..., None] + jnp.einsum('bhqk,bkr->bhqr', p, ck.astype(jnp.float32))
        return (m_new, l, acc), None

    (m, l, acc), _ = lax.scan(step, (m0, l0, a0), page_table.T)
    s = (jnp.einsum('bqhr,bkr->bhqk', q_lat, c_kv_new)
         + jnp.einsum('bqhd,bkd->bhqk', q_pe, k_pe_new)).astype(jnp.float32) * SOFTMAX_SCALE
    causal = jnp.tril(jnp.ones((Sd, Sd), dtype=bool))
    s = jnp.where(causal, s, NEG)
    m_tot = jnp.maximum(m, s.max(-1))
    corr = jnp.exp(m - m_tot)
    p = jnp.exp(s - m_tot[..., None])
    l = l * corr + p.sum(-1)
    acc = acc * corr[..., None] + jnp.einsum('bhqk,bkr->bhqr', p, c_kv_new.astype(jnp.float32))
    return (acc / l[..., None]).transpose(0, 2, 1, 3).astype(q_lat.dtype)


def chunk_mix(u, v, g_v, w_s, b_s):
    B, S = v.shape[0], v.shape[1]
    v = rmsnorm(v, g_v)
    nc = -(-S // CHUNK)
    pad = nc * CHUNK - S
    vp = jnp.pad(v, ((0, 0), (0, pad), (0, 0))).reshape(B, nc, CHUNK, CM_GROUPS, CM_HEAD)
    ws = w_s * jnp.tril(jnp.ones((CHUNK, CHUNK), dtype=w_s.dtype))[None]
    mixed = jnp.einsum('gij,bcjgd->bcigd', ws, vp) + b_s.T[None, None, :, :, None]
    mixed = mixed.reshape(B, nc * CHUNK, CM_WIDTH)[:, :S]
    last_start = ((S - 1) // CHUNK) * CHUNK
    return u * mixed, v[:, last_start:]


def layer(h, p, pos, attend, g_mix, w_in, g_q, w_uq, g_kv, w_uk, w_uv, g_v, w_s, b_s, w_o,
          g_ffn, w_ffn_gate, w_ffn_up, w_ffn_down, g_ple, w_ple_gate, w_ple_proj):
    B, S = h.shape[0], h.shape[1]
    hn = rmsnorm(h, g_mix)
    q_lat, q_pe, c_kv, k_pe, u, v = mixer_projections(hn, pos, w_in, g_q, w_uq, g_kv, w_uk)
    o_lat = attend(q_lat, q_pe, c_kv, k_pe)
    o_attn = jnp.einsum('bshr,rhd->bshd', o_lat, w_uv).reshape(B, S, MLA_WIDTH)
    o_cm, v_state = chunk_mix(u, v, g_v, w_s, b_s)
    h = h + jnp.concatenate([o_attn, o_cm], axis=-1) @ w_o
    hf = rmsnorm(h, g_ffn)
    h = h + (jax.nn.silu(hf @ w_ffn_gate) * (hf @ w_ffn_up)) @ w_ffn_down
    h = h + jax.nn.sigmoid(rmsnorm(h, g_ple) @ w_ple_gate) * (p @ w_ple_proj)
    return h, c_kv, k_pe, v_state


def setup_inputs(seed: int = 0) -> dict:
    key = jax.random.key(seed)
    ks = jax.random.split(key, 32)
    f32 = jnp.float32
    n_pages = PAST_LEN // PAGE_SIZE
    n_used = DEC_BATCH * n_pages
    n_pool = n_used + n_used // 4

    def nrm(k, shape, scale):
        return jax.random.normal(k, shape, f32) * scale

    def gain(k, shape):
        return 1.0 + 0.05 * jax.random.normal(k, shape, f32)

    page_table = jax.random.permutation(ks[0], n_pool)[:n_used].reshape(DEC_BATCH, n_pages).astype(jnp.int32)
    return {
        "x_prompt": nrm(ks[1], (BATCH, SEQ, D_MODEL), 1.0),
        "x_sample": nrm(ks[2], (DEC_BATCH, DEC_SEQ, D_MODEL), 1.0),
        "p_prompt": nrm(ks[3], (DEPTH, BATCH, SEQ, PLE_DIM), 1.0),
        "p_sample": nrm(ks[4], (DEPTH, DEC_BATCH, DEC_SEQ, PLE_DIM), 1.0),
        "cache_ckv": nrm(ks[5], (DEPTH, n_pool, PAGE_SIZE, KV_RANK), 1.0),
        "cache_kpe": nrm(ks[6], (DEPTH, n_pool, PAGE_SIZE, ROPE_DIM), 1.0),
        "page_table": page_table,
        "g_mix": gain(ks[7], (DEPTH, D_MODEL)),
        "w_in": nrm(ks[8], (DEPTH, D_MODEL, IN_COLS), D_MODEL ** -0.5),
        "g_q": gain(ks[9], (DEPTH, Q_RANK)),
        "w_uq": nrm(ks[10], (DEPTH, Q_RANK, MLA_HEADS, NOPE_DIM + ROPE_DIM), Q_RANK ** -0.5),
        "g_kv": gain(ks[11], (DEPTH, KV_RANK)),
        "w_uk": nrm(ks[12], (DEPTH, KV_RANK, MLA_HEADS, NOPE_DIM), KV_RANK ** -0.5),
        "w_uv": nrm(ks[13], (DEPTH, KV_RANK, MLA_HEADS, V_HEAD_DIM), KV_RANK ** -0.5),
        "g_v": gain(ks[14], (DEPTH, CM_WIDTH)),
        "w_s": nrm(ks[15], (DEPTH, CM_GROUPS, CHUNK, CHUNK), CHUNK ** -0.5),
        "b_s": 1.0 + nrm(ks[16], (DEPTH, CM_GROUPS, CHUNK), 0.05),
        "w_o": nrm(ks[17], (DEPTH, MIX_WIDTH, D_MODEL), 0.5 * MIX_WIDTH ** -0.5),
        "g_ffn": gain(ks[18], (DEPTH, D_MODEL)),
        "w_ffn_gate": nrm(ks[19], (DEPTH, D_MODEL, FFN_HIDDEN), D_MODEL ** -0.5),
        "w_ffn_up": nrm(ks[20], (DEPTH, D_MODEL, FFN_HIDDEN), D_MODEL ** -0.5),
        "w_ffn_down": nrm(ks[21], (DEPTH, FFN_HIDDEN, D_MODEL), 0.5 * FFN_HIDDEN ** -0.5),
        "g_ple": gain(ks[22], (DEPTH, D_MODEL)),
        "w_ple_gate": nrm(ks[23], (DEPTH, D_MODEL, D_MODEL), D_MODEL ** -0.5),
        "w_ple_proj": nrm(ks[24], (DEPTH, PLE_DIM, D_MODEL), 0.5 * PLE_DIM ** -0.5),
        "g_final": gain(ks[25], (D_MODEL,)),
    }


def reference(x_prompt, x_sample, p_prompt, p_sample, cache_ckv, cache_kpe, page_table,
              g_mix, w_in, g_q, w_uq, g_kv, w_uk, w_uv, g_v, w_s, b_s, w_o,
              g_ffn, w_ffn_gate, w_ffn_up, w_ffn_down, g_ple, w_ple_gate, w_ple_proj, g_final):
    pos_p = jnp.arange(x_prompt.shape[1])
    pos_s = PAST_LEN + jnp.arange(x_sample.shape[1])
    hp, hs = x_prompt, x_sample
    ckv_p, kpe_p, cmv_p, ckv_s, kpe_s, cmv_s = [], [], [], [], [], []
    for i in range(DEPTH):
        lw = (g_mix[i], w_in[i], g_q[i], w_uq[i], g_kv[i], w_uk[i], w_uv[i], g_v[i], w_s[i], b_s[i],
              w_o[i], g_ffn[i], w_ffn_gate[i], w_ffn_up[i], w_ffn_down[i], g_ple[i],
              w_ple_gate[i], w_ple_proj[i])
        cache_ckv_i, cache_kpe_i = cache_ckv[i], cache_kpe[i]

        def attend_sample(ql, qp, c, k, cc=cache_ckv_i, cp=cache_kpe_i):
            return mla_sample(ql, qp, c, k, cc, cp, page_table)

        hp, c1, k1, v1 = layer(hp, p_prompt[i], pos_p, mla_prompt, *lw)
        hs, c2, k2, v2 = layer(hs, p_sample[i], pos_s, attend_sample, *lw)
        ckv_p.append(c1); kpe_p.append(k1); cmv_p.append(v1)
        ckv_s.append(c2); kpe_s.append(k2); cmv_s.append(v2)
    y_prompt = rmsnorm(hp, g_final)
    y_sample = rmsnorm(hs, g_final)
    return (y_prompt, y_sample,
            jnp.stack(ckv_p), jnp.stack(kpe_p), jnp.stack(cmv_p),
            jnp.stack(ckv_s), jnp.stack(kpe_s), jnp.stack(cmv_s))
```

```python
import functools
import math

import jax
import jax.numpy as jnp
from jax import lax
from jax.experimental import pallas as pl
from jax.experimental.pallas import tpu as pltpu

F32 = jnp.float32
BF16 = jnp.bfloat16

EPS = 1e-6
NEG = -1e30
ROPE_THETA = 10000.0
LANES = 128
SUBLANES = 8
VMEM_LIMIT = 56 * 1024 * 1024


def _params(*sem):
    return pltpu.CompilerParams(dimension_semantics=sem, vmem_limit_bytes=VMEM_LIMIT)


def _rms(x, g):
    return x * lax.rsqrt(jnp.mean(x * x, axis=-1, keepdims=True) + EPS) * g


def _dot(a, b):
    return jnp.dot(a, b, preferred_element_type=F32)


def _dot_nt(a, b):
    return lax.dot_general(a, b, (((1,), (1,)), ((), ())), preferred_element_type=F32)


def _dot_tn(a, b):
    return lax.dot_general(a, b, (((0,), (0,)), ((), ())), preferred_element_type=F32)


def _const_spec(shape):
    nd = len(shape)
    return pl.BlockSpec(shape, lambda *_: (0,) * nd)


def _in_proj_kernel(x_ref, gmix_ref, wa_ref, wpe_ref, wcm_ref, gq_ref, gkv_ref, gv_ref,
                    inv_ref, sign_ref, ws_ref, bs_ref,
                    cq_ref, ckv_ref, kpe_ref, ckvb_ref, kpeb_ref, ocm_ref, vn_ref,
                    cos_ref, sin_ref, *, bm, seq_len, pos0, q_rank, rope_dim, cm_width,
                    chunk, groups):
    i = pl.program_id(0)
    hn = _rms(x_ref[...], gmix_ref[...]).astype(BF16)

    za = _dot(hn, wa_ref[...])
    cq_ref[...] = _rms(za[:, :q_rank], gq_ref[...]).astype(BF16)
    ckv = _rms(za[:, q_rank:], gkv_ref[...])
    ckv_ref[...] = ckv
    ckvb_ref[...] = ckv.astype(BF16)

    row = i * bm + lax.broadcasted_iota(jnp.int32, (bm, LANES), 0)
    pos = pos0 + lax.rem(row, seq_len)
    ang = pos.astype(F32) * inv_ref[...]
    c = jnp.cos(ang)
    s = jnp.sin(ang) * sign_ref[...]
    cos_ref[...] = c
    sin_ref[...] = s
    zpe = _dot(hn, wpe_ref[...])
    kpe = (zpe * c + pltpu.roll(zpe, rope_dim // 2, axis=1) * s)[:, :rope_dim]
    kpe_ref[...] = kpe
    kpeb_ref[...] = kpe.astype(BF16)

    gz = jax.nn.gelu(_dot(hn, wcm_ref[...]))
    u = gz[:, :cm_width]
    vn = _rms(gz[:, cm_width:], gv_ref[...])
    vn_ref[...] = vn
    if seq_len % chunk == 0:
        vb = vn.astype(BF16)
        r = lax.broadcasted_iota(jnp.int32, (chunk, chunk), 0)
        cc = lax.broadcasted_iota(jnp.int32, (chunk, chunk), 1)
        head = cm_width // groups
        for g in range(groups):
            wsg = jnp.where(r >= cc, ws_ref[g], 0.0).astype(BF16)
            bcol = bs_ref[:, g:g + 1]
            for k in range(bm // chunk):
                rows = slice(k * chunk, (k + 1) * chunk)
                cols = slice(g * head, (g + 1) * head)
                mixed = _dot(wsg, vb[rows, cols]) + bcol
                ocm_ref[rows, cols] = (u[rows, cols] * mixed).astype(BF16)
    else:
        ocm_ref[...] = (u * (vn * ws_ref[...] + bs_ref[...])).astype(BF16)


def _in_proj(x, w, *, bm, seq_len, pos0, dims):
    n, d = x.shape
    q_rank, kv_rank, rope_dim, cm_width, chunk, groups = dims
    chunked = seq_len % chunk == 0
    assert chunked or seq_len == 1
    if chunked:
        ws, bs = w["w_s"], w["b_sT"]
    else:
        ws, bs = w["w_s_diag"], w["b_s_diag"]
    kern = functools.partial(_in_proj_kernel, bm=bm, seq_len=seq_len, pos0=pos0, q_rank=q_rank,
                             rope_dim=rope_dim, cm_width=cm_width, chunk=chunk, groups=groups)
    row = lambda width: pl.BlockSpec((bm, width), lambda i: (i, 0))
    consts = [w["g_mix"], w["w_a"], w["w_pe"], w["w_cm"], w["g_q"], w["g_kv"], w["g_v"],
              w["inv128"], w["sign128"], ws, bs]
    out_shape = (
        jax.ShapeDtypeStruct((n, q_rank), BF16),
        jax.ShapeDtypeStruct((n, kv_rank), F32),
        jax.ShapeDtypeStruct((n, rope_dim), F32),
        jax.ShapeDtypeStruct((n, kv_rank), BF16),
        jax.ShapeDtypeStruct((n, rope_dim), BF16),
        jax.ShapeDtypeStruct((n, cm_width), BF16),
        jax.ShapeDtypeStruct((n, cm_width), F32),
        jax.ShapeDtypeStruct((n, LANES), F32),
        jax.ShapeDtypeStruct((n, LANES), F32),
    )
    return pl.pallas_call(
        kern,
        grid=(n // bm,),
        in_specs=[row(d)] + [_const_spec(a.shape) for a in consts],
        out_specs=tuple(row(s.shape[1]) for s in out_shape),
        out_shape=out_shape,
        compiler_params=_params("parallel"),
        name="in_proj",
    )(x, *consts)


def _q_proj_kernel(cq_ref, wuq_ref, wuk_ref, cos_ref, sin_ref, qlat_ref, qpe_ref, *,
                   heads, nope, rope_dim):
    q = _dot(cq_ref[...], wuq_ref[...])
    qn = q[:, :heads * nope].astype(BF16)
    for h in range(heads):
        qlat_ref[h] = _dot(qn[:, h * nope:(h + 1) * nope], wuk_ref[h]).astype(BF16)
    qp = q[:, heads * nope:]
    width = heads * rope_dim
    reps = width // LANES
    c = jnp.concatenate([cos_ref[...]] * reps, axis=1)
    s = jnp.concatenate([sin_ref[...]] * reps, axis=1)
    half = rope_dim // 2
    lane = lax.broadcasted_iota(jnp.int32, qp.shape, 1)
    first_half = lax.rem(lane, rope_dim) < half
    partner = jnp.where(first_half, pltpu.roll(qp, width - half, axis=1),
                        pltpu.roll(qp, half, axis=1))
    qr = qp * c + partner * s
    for h in range(heads):
        qpe_ref[h] = qr[:, h * rope_dim:(h + 1) * rope_dim].astype(BF16)


def _q_proj(cq, cos, sin, w, *, bm, dims):
    n, q_rank = cq.shape
    heads, nope, rope_dim, kv_rank = dims
    kern = functools.partial(_q_proj_kernel, heads=heads, nope=nope, rope_dim=rope_dim)
    row = lambda width: pl.BlockSpec((bm, width), lambda i: (i, 0))
    return pl.pallas_call(
        kern,
        grid=(n // bm,),
        in_specs=[row(q_rank), _const_spec(w["w_uq"].shape), _const_spec(w["w_ukT"].shape),
                  row(LANES), row(LANES)],
        out_specs=(pl.BlockSpec((heads, bm, kv_rank), lambda i: (0, i, 0)),
                   pl.BlockSpec((heads, bm, rope_dim), lambda i: (0, i, 0))),
        out_shape=(jax.ShapeDtypeStruct((heads, n, kv_rank), BF16),
                   jax.ShapeDtypeStruct((heads, n, rope_dim), BF16)),
        compiler_params=_params("parallel"),
        name="q_proj",
    )(cq, w["w_uq"], w["w_ukT"], cos, sin)


def _attn_prompt_kernel(qlat_ref, qpe_ref, kc_ref, kp_ref, wuv_ref, o_ref,
                        m_ref, l_ref, acc_ref, *, tq, tk, heads, scale):
    qi = pl.program_id(1)
    kv_rank = qlat_ref.shape[-1]
    rope_dim = qpe_ref.shape[-1]
    vdim = wuv_ref.shape[-1]
    q = qlat_ref[...].reshape(heads * tq, kv_rank)
    qp = qpe_ref[...].reshape(heads * tq, rope_dim)
    m_ref[...] = jnp.full(m_ref.shape, NEG, F32)
    l_ref[...] = jnp.zeros(l_ref.shape, F32)
    acc_ref[...] = jnp.zeros(acc_ref.shape, F32)
    qpos = qi * tq + lax.rem(lax.broadcasted_iota(jnp.int32, (heads * tq, tk), 0), tq)
    col = lax.broadcasted_iota(jnp.int32, (heads * tq, tk), 1)
    n_kb = (qi * tq + tq + tk - 1) // tk

    def body(kb, carry):
        k0 = pl.multiple_of(kb * tk, tk)
        kc = kc_ref[pl.ds(k0, tk), :]
        kp = kp_ref[pl.ds(k0, tk), :]
        s = (_dot_nt(q, kc) + _dot_nt(qp, kp)) * scale
        s = jnp.where(k0 + col <= qpos, s, NEG)
        m_old = m_ref[...]
        m_new = jnp.maximum(m_old, jnp.max(s, axis=1, keepdims=True))
        corr = jnp.exp(m_old - m_new)
        p = jnp.exp(s - m_new)
        l_ref[...] = l_ref[...] * corr + jnp.sum(p, axis=1, keepdims=True)
        acc_ref[...] = acc_ref[...] * corr + _dot(p.astype(BF16), kc)
        m_ref[...] = m_new
        return carry

    lax.fori_loop(0, n_kb, body, 0)
    o = (acc_ref[...] / l_ref[...]).astype(BF16)
    for h in range(heads):
        o_ref[:, h * vdim:(h + 1) * vdim] = _dot(o[h * tq:(h + 1) * tq], wuv_ref[h]).astype(BF16)


def _attn_prompt(qlat, qpe, kc, kp, wuv, *, batch, seq, tq, tk, scale):
    heads, n, kv_rank = qlat.shape
    rope_dim = qpe.shape[-1]
    vdim = wuv.shape[-1]
    nq = seq // tq
    kern = functools.partial(_attn_prompt_kernel, tq=tq, tk=tk, heads=heads, scale=scale)
    return pl.pallas_call(
        kern,
        grid=(batch, nq),
        in_specs=[pl.BlockSpec((heads, tq, kv_rank), lambda b, i: (0, b * nq + i, 0)),
                  pl.BlockSpec((heads, tq, rope_dim), lambda b, i: (0, b * nq + i, 0)),
                  pl.BlockSpec((seq, kv_rank), lambda b, i: (b, 0)),
                  pl.BlockSpec((seq, rope_dim), lambda b, i: (b, 0)),
                  _const_spec(wuv.shape)],
        out_specs=pl.BlockSpec((tq, heads * vdim), lambda b, i: (b * nq + i, 0)),
        out_shape=jax.ShapeDtypeStruct((n, heads * vdim), BF16),
        scratch_shapes=[pltpu.VMEM((heads * tq, 1), F32), pltpu.VMEM((heads * tq, 1), F32),
                        pltpu.VMEM((heads * tq, kv_rank), F32)],
        compiler_params=_params("parallel", "arbitrary"),
        name="attn_prompt",
    )(qlat, qpe, kc, kp, wuv)


def _row_to_col(v, rows):
    r = lax.broadcasted_iota(jnp.int32, (rows, LANES), 0)
    c = lax.broadcasted_iota(jnp.int32, (rows, LANES), 1)
    return jnp.sum(jnp.where(r == c, v, 0.0), axis=1, keepdims=True)


def _attn_sample_kernel(pt_ref, qlat_ref, qpe_ref, cnew_ref, knew_ref, *rest, pps, heads, scale):
    ck_refs = rest[:pps]
    kp_refs = rest[pps:2 * pps]
    o_ref = rest[2 * pps]
    qpad_ref, qpepad_ref, m_ref, l_ref, acc_ref = rest[2 * pps + 1:]
    step = pl.program_id(1)
    kv_rank = qlat_ref.shape[-1]
    rope_dim = qpe_ref.shape[-1]

    @pl.when(step == 0)
    def _():
        qpad_ref[...] = jnp.concatenate(
            [qlat_ref[...], jnp.zeros((LANES - heads, kv_rank), BF16)], axis=0)
        qpepad_ref[...] = jnp.concatenate(
            [qpe_ref[...], jnp.zeros((LANES - heads, rope_dim), BF16)], axis=0)
        m_ref[...] = jnp.full(m_ref.shape, NEG, F32)
        l_ref[...] = jnp.zeros(l_ref.shape, F32)
        acc_ref[...] = jnp.zeros(acc_ref.shape, F32)

    kc = jnp.concatenate([r[...].astype(BF16) for r in ck_refs], axis=0)
    kp = jnp.concatenate([r[...].astype(BF16) for r in kp_refs], axis=0)
    s = (_dot_nt(kc, qpad_ref[...]) + _dot_nt(kp, qpepad_ref[...])) * scale
    m_old = m_ref[...]
    m_new = jnp.maximum(m_old, jnp.max(s, axis=0, keepdims=True))
    corr = jnp.exp(m_old - m_new)
    p = jnp.exp(s - m_new)
    l_ref[...] = l_ref[...] * corr + jnp.sum(p, axis=0, keepdims=True)
    pv = _dot_tn(p.astype(BF16), kc)
    acc_ref[...] = acc_ref[...] * _row_to_col(corr, heads) + pv[:heads]
    m_ref[...] = m_new

    @pl.when(step == pl.num_programs(1) - 1)
    def _():
        cn = cnew_ref[...]
        s_new = (jnp.sum(qlat_ref[...].astype(F32) * cn, axis=1, keepdims=True)
                 + jnp.sum(qpe_ref[...].astype(F32) * knew_ref[...], axis=1, keepdims=True)) * scale
        m_col = _row_to_col(m_ref[...], heads)
        l_col = _row_to_col(l_ref[...], heads)
        m_tot = jnp.maximum(m_col, s_new)
        c_old = jnp.exp(m_col - m_tot)
        p_new = jnp.exp(s_new - m_tot)
        l_tot = l_col * c_old + p_new
        o_ref[...] = (acc_ref[...] * c_old + p_new * cn) / l_tot


def _attn_sample(qlat, qpe, cnew, knew, cache_ckv, cache_kpe, page_table, *, pps, scale):
    bd, heads, kv_rank = qlat.shape
    rope_dim = qpe.shape[-1]
    n_pages = page_table.shape[1]
    page = cache_ckv.shape[1]
    assert n_pages % pps == 0
    pt = page_table.reshape(-1)
    kern = functools.partial(_attn_sample_kernel, pps=pps, heads=heads, scale=scale)

    def page_spec(width, j):
        return pl.BlockSpec((None, page, width),
                            lambda b, s, pt_ref: (pt_ref[b * n_pages + s * pps + j], 0, 0))

    per_b = lambda r, width: pl.BlockSpec((None, r, width), lambda b, s, pt_ref: (b, 0, 0))
    grid_spec = pltpu.PrefetchScalarGridSpec(
        num_scalar_prefetch=1,
        grid=(bd, n_pages // pps),
        in_specs=([per_b(heads, kv_rank), per_b(heads, rope_dim), per_b(1, kv_rank),
                   per_b(1, rope_dim)]
                  + [page_spec(kv_rank, j) for j in range(pps)]
                  + [page_spec(rope_dim, j) for j in range(pps)]),
        out_specs=per_b(heads, kv_rank),
        scratch_shapes=[pltpu.VMEM((LANES, kv_rank), BF16), pltpu.VMEM((LANES, rope_dim), BF16),
                        pltpu.VMEM((1, LANES), F32), pltpu.VMEM((1, LANES), F32),
                        pltpu.VMEM((heads, kv_rank), F32)],
    )
    return pl.pallas_call(
        kern,
        grid_spec=grid_spec,
        out_shape=jax.ShapeDtypeStruct((bd, heads, kv_rank), F32),
        compiler_params=_params("parallel", "arbitrary"),
        name="attn_sample",
    )(pt, qlat, qpe, cnew, knew, *([cache_ckv] * pps), *([cache_kpe] * pps))


def _uv_kernel(o_ref, wuv_ref, out_ref, *, heads):
    kv_rank = wuv_ref.shape[1]
    vdim = wuv_ref.shape[2]
    for h in range(heads):
        oh = o_ref[:, h * kv_rank:(h + 1) * kv_rank].astype(BF16)
        out_ref[:, h * vdim:(h + 1) * vdim] = _dot(oh, wuv_ref[h]).astype(BF16)


def _uv_proj(o_lat, wuv):
    n = o_lat.shape[0]
    heads, kv_rank, vdim = wuv.shape
    return pl.pallas_call(
        functools.partial(_uv_kernel, heads=heads),
        grid=(1,),
        in_specs=[_const_spec(o_lat.shape), _const_spec(wuv.shape)],
        out_specs=_const_spec((n, heads * vdim)),
        out_shape=jax.ShapeDtypeStruct((n, heads * vdim), BF16),
        compiler_params=_params("arbitrary"),
        name="uv_proj",
    )(o_lat, wuv)


def _out_proj_kernel(x_ref, oa_ref, ocm_ref, wo_ref, gffn_ref, h_ref, hf_ref):
    wa = oa_ref.shape[1]
    h1 = x_ref[...] + _dot(oa_ref[...], wo_ref[:wa, :]) + _dot(ocm_ref[...], wo_ref[wa:, :])
    h_ref[...] = h1
    hf_ref[...] = _rms(h1, gffn_ref[...]).astype(BF16)


def _out_proj(x, oa, ocm, w, *, bm):
    n, d = x.shape
    row = lambda width: pl.BlockSpec((bm, width), lambda i: (i, 0))
    return pl.pallas_call(
        _out_proj_kernel,
        grid=(n // bm,),
        in_specs=[row(d), row(oa.shape[1]), row(ocm.shape[1]), _const_spec(w["w_o"].shape),
                  _const_spec(w["g_ffn"].shape)],
        out_specs=(row(d), row(d)),
        out_shape=(jax.ShapeDtypeStruct((n, d), F32), jax.ShapeDtypeStruct((n, d), BF16)),
        compiler_params=_params("parallel"),
        name="out_proj",
    )(x, oa, ocm, w["w_o"], w["g_ffn"])


def _ffn_kernel(h_ref, hf_ref, wg_ref, wu_ref, wd_ref, out_ref):
    j = pl.program_id(1)

    @pl.when(j == 0)
    def _():
        out_ref[...] = h_ref[...]

    hf = hf_ref[...]
    t = (jax.nn.silu(_dot(hf, wg_ref[...])) * _dot(hf, wu_ref[...])).astype(BF16)
    out_ref[...] += _dot(t, wd_ref[...])


def _ffn(h, hf, w, *, bm, bh):
    n, d = h.shape
    hidden = w["w_gate"].shape[1]
    assert hidden % bh == 0
    return pl.pallas_call(
        _ffn_kernel,
        grid=(n // bm, hidden // bh),
        in_specs=[pl.BlockSpec((bm, d), lambda i, j: (i, 0)),
                  pl.BlockSpec((bm, d), lambda i, j: (i, 0)),
                  pl.BlockSpec((d, bh), lambda i, j: (0, j)),
                  pl.BlockSpec((d, bh), lambda i, j: (0, j)),
                  pl.BlockSpec((bh, d), lambda i, j: (j, 0))],
        out_specs=pl.BlockSpec((bm, d), lambda i, j: (i, 0)),
        out_shape=jax.ShapeDtypeStruct((n, d), F32),
        compiler_params=_params("parallel", "arbitrary"),
        name="ffn",
    )(h, hf, w["w_gate"], w["w_up"], w["w_down"])


def _ple_kernel(h_ref, p_ref, gple_ref, wg_ref, wp_ref, gfin_ref, y_ref, *, final):
    h = h_ref[...]
    gate = jax.nn.sigmoid(_dot(_rms(h, gple_ref[...]).astype(BF16), wg_ref[...]))
    h3 = h + gate * _dot(p_ref[...].astype(BF16), wp_ref[...])
    y_ref[...] = _rms(h3, gfin_ref[...]) if final else h3


def _ple(h, p, w, g_final, *, bm, final):
    n, d = h.shape
    row = lambda width: pl.BlockSpec((bm, width), lambda i: (i, 0))
    return pl.pallas_call(
        functools.partial(_ple_kernel, final=final),
        grid=(n // bm,),
        in_specs=[row(d), row(p.shape[1]), _const_spec(w["g_ple"].shape),
                  _const_spec(w["w_ple_gate"].shape), _const_spec(w["w_ple_proj"].shape),
                  _const_spec(g_final.shape)],
        out_specs=row(d),
        out_shape=jax.ShapeDtypeStruct((n, d), F32),
        compiler_params=_params("parallel"),
        name="ple",
    )(h, p, w["g_ple"], w["w_ple_gate"], w["w_ple_proj"], g_final)


def _prep_weights(i, g_mix, w_in, g_q, w_uq, g_kv, w_uk, w_uv, g_v, w_s, b_s, w_o, g_ffn,
                  w_ffn_gate, w_ffn_up, w_ffn_down, g_ple, w_ple_gate, w_ple_proj):
    q_rank = g_q.shape[1]
    kv_rank = g_kv.shape[1]
    heads, qk_dim = w_uq.shape[2], w_uq.shape[3]
    nope = w_uk.shape[3]
    rope_dim = qk_dim - nope
    groups = w_s.shape[1]
    cm_width = g_v.shape[1]
    wi = w_in[i]
    pe = wi[:, q_rank + kv_rank:q_rank + kv_rank + rope_dim]
    half = rope_dim // 2
    inv = ROPE_THETA ** (-jnp.arange(half, dtype=F32) / half)
    lane = jnp.arange(LANES)
    return {
        "g_mix": g_mix[i][None], "g_q": g_q[i][None], "g_kv": g_kv[i][None], "g_v": g_v[i][None],
        "g_ffn": g_ffn[i][None], "g_ple": g_ple[i][None],
        "w_a": wi[:, :q_rank + kv_rank].astype(BF16),
        "w_pe": jnp.concatenate([pe] * (LANES // rope_dim), axis=1).astype(BF16),
        "w_cm": wi[:, q_rank + kv_rank + rope_dim:].astype(BF16),
        "inv128": jnp.tile(inv, LANES // half)[None],
        "sign128": jnp.where((lane % rope_dim) < half, -1.0, 1.0).astype(F32)[None],
        "w_s": w_s[i], "b_sT": b_s[i].T,
        "w_s_diag": jnp.repeat(w_s[i][:, 0, 0], cm_width // groups)[None],
        "b_s_diag": jnp.repeat(b_s[i][:, 0], cm_width // groups)[None],
        "w_uq": jnp.concatenate([w_uq[i][:, :, :nope].reshape(q_rank, heads * nope),
                                 w_uq[i][:, :, nope:].reshape(q_rank, heads * rope_dim)],
                                axis=1).astype(BF16),
        "w_ukT": jnp.transpose(w_uk[i], (1, 2, 0)).astype(BF16),
        "w_uvT": jnp.transpose(w_uv[i], (1, 0, 2)).astype(BF16),
        "w_o": w_o[i].astype(BF16),
        "w_gate": w_ffn_gate[i].astype(BF16), "w_up": w_ffn_up[i].astype(BF16),
        "w_down": w_ffn_down[i].astype(BF16),
        "w_ple_gate": w_ple_gate[i].astype(BF16), "w_ple_proj": w_ple_proj[i].astype(BF16),
    }


def kernel(x_prompt, x_sample, p_prompt, p_sample, cache_ckv, cache_kpe, page_table, g_mix, w_in,
           g_q, w_uq, g_kv, w_uk, w_uv, g_v, w_s, b_s, w_o, g_ffn, w_ffn_gate, w_ffn_up,
           w_ffn_down, g_ple, w_ple_gate, w_ple_proj, g_final):
    batch, seq, d_model = x_prompt.shape
    dec_batch, dec_seq, _ = x_sample.shape
    depth = g_mix.shape[0]
    q_rank, kv_rank = g_q.shape[1], g_kv.shape[1]
    heads, qk_dim = w_uq.shape[2], w_uq.shape[3]
    nope = w_uk.shape[3]
    rope_dim = qk_dim - nope
    cm_width = g_v.shape[1]
    groups, chunk = w_s.shape[1], w_s.shape[2]
    page = cache_ckv.shape[2]
    past_len = page_table.shape[1] * page
    scale = 1.0 / math.sqrt(nope + rope_dim)
    assert dec_seq == 1 and seq % chunk == 0
    in_dims = (q_rank, kv_rank, rope_dim, cm_width, chunk, groups)
    q_dims = (heads, nope, rope_dim, kv_rank)
    gfin = g_final[None]

    hp = x_prompt.reshape(batch * seq, d_model)
    hs = x_sample.reshape(dec_batch * dec_seq, d_model)
    outs = [[] for _ in range(6)]
    for i in range(depth):
        w = _prep_weights(i, g_mix, w_in, g_q, w_uq, g_kv, w_uk, w_uv, g_v, w_s, b_s, w_o, g_ffn,
                          w_ffn_gate, w_ffn_up, w_ffn_down, g_ple, w_ple_gate, w_ple_proj)

        cq, ckv, kpe, ckvb, kpeb, ocm, vn, cos, sin = _in_proj(
            hp, w, bm=512, seq_len=seq, pos0=0, dims=in_dims)
        qlat, qpe = _q_proj(cq, cos, sin, w, bm=512, dims=q_dims)
        oa = _attn_prompt(qlat, qpe, ckvb, kpeb, w["w_uvT"], batch=batch, seq=seq, tq=128,
                          tk=512, scale=scale)
        h1, hf = _out_proj(hp, oa, ocm, w, bm=512)
        h2 = _ffn(h1, hf, w, bm=512, bh=512)
        hp = _ple(h2, p_prompt[i].reshape(batch * seq, -1), w, gfin, bm=512,
                  final=i == depth - 1)
        outs[0].append(ckv.reshape(batch, seq, kv_rank))
        outs[1].append(kpe.reshape(batch, seq, rope_dim))
        outs[2].append(vn.reshape(batch, seq, cm_width)[:, seq - chunk:])

        cq, ckv, kpe, ckvb, kpeb, ocm, vn, cos, sin = _in_proj(
            hs, w, bm=dec_batch, seq_len=dec_seq, pos0=past_len, dims=in_dims)
        qlat, qpe = _q_proj(cq, cos, sin, w, bm=dec_batch, dims=q_dims)
        o_lat = _attn_sample(jnp.transpose(qlat, (1, 0, 2)), jnp.transpose(qpe, (1, 0, 2)),
                             ckv[:, None, :], kpe[:, None, :], cache_ckv[i], cache_kpe[i],
                             page_table, pps=16, scale=scale)
        oa = _uv_proj(o_lat.reshape(dec_batch, heads * kv_rank), w["w_uvT"])
        h1, hf = _out_proj(hs, oa, ocm, w, bm=dec_batch)
        h2 = _ffn(h1, hf, w, bm=dec_batch, bh=512)
        hs = _ple(h2, p_sample[i].reshape(dec_batch * dec_seq, -1), w, gfin, bm=dec_batch,
                  final=i == depth - 1)
        outs[3].append(ckv.reshape(dec_batch, dec_seq, kv_rank))
        outs[4].append(kpe.reshape(dec_batch, dec_seq, rope_dim))
        outs[5].append(vn.reshape(dec_batch, dec_seq, cm_width))

    y_prompt = hp.reshape(batch, seq, d_model)
    y_sample = hs.reshape(dec_batch, dec_seq, d_model)
    return (y_prompt, y_sample) + tuple(jnp.stack(o) for o in outs)
```

```python
import functools
import math

import jax
import jax.numpy as jnp
from jax import lax
from jax.experimental import pallas as pl
from jax.experimental.pallas import tpu as pltpu

F32 = jnp.float32
BF16 = jnp.bfloat16

EPS = 1e-6
NEG = -1e30
ROPE_THETA = 10000.0
LANES = 128
SUBLANES = 8
VMEM_LIMIT = 56 * 1024 * 1024


def _params(*sem):
    return pltpu.CompilerParams(dimension_semantics=sem, vmem_limit_bytes=VMEM_LIMIT)


def _rms(x, g):
    return x * lax.rsqrt(jnp.mean(x * x, axis=-1, keepdims=True) + EPS) * g


def _dot(a, b):
    return jnp.dot(a, b, preferred_element_type=F32)


def _dot_nt(a, b):
    return lax.dot_general(a, b, (((1,), (1,)), ((), ())), preferred_element_type=F32)


def _const_spec(shape):
    nd = len(shape)
    return pl.BlockSpec(shape, lambda *_: (0,) * nd)


def _in_proj_kernel(x_ref, gmix_ref, wa_ref, wpe_ref, wcm_ref, gq_ref, gkv_ref, gv_ref,
                    inv_ref, sign_ref, ws_ref, bs_ref,
                    cq_ref, ckv_ref, kpe_ref, ckvb_ref, kpeb_ref, ocm_ref, vn_ref,
                    cos_ref, sin_ref, *, bm, seq_len, pos0, q_rank, rope_dim, cm_width,
                    chunk, groups):
    i = pl.program_id(0)
    hn = _rms(x_ref[...], gmix_ref[...]).astype(BF16)

    za = _dot(hn, wa_ref[...])
    cq_ref[...] = _rms(za[:, :q_rank], gq_ref[...]).astype(BF16)
    ckv = _rms(za[:, q_rank:], gkv_ref[...])
    ckv_ref[...] = ckv
    ckvb_ref[...] = ckv.astype(BF16)

    row = i * bm + lax.broadcasted_iota(jnp.int32, (bm, LANES), 0)
    pos = pos0 + lax.rem(row, seq_len)
    ang = pos.astype(F32) * inv_ref[...]
    c = jnp.cos(ang)
    s = jnp.sin(ang) * sign_ref[...]
    cos_ref[...] = c
    sin_ref[...] = s
    zpe = _dot(hn, wpe_ref[...])
    kpe = (zpe * c + pltpu.roll(zpe, rope_dim // 2, axis=1) * s)[:, :rope_dim]
    kpe_ref[...] = kpe
    kpeb_ref[...] = kpe.astype(BF16)

    gz = jax.nn.gelu(_dot(hn, wcm_ref[...]))
    u = gz[:, :cm_width]
    vn = _rms(gz[:, cm_width:], gv_ref[...])
    vn_ref[...] = vn
    if seq_len % chunk == 0:
        vb = vn.astype(BF16)
        r = lax.broadcasted_iota(jnp.int32, (chunk, chunk), 0)
        cc = lax.broadcasted_iota(jnp.int32, (chunk, chunk), 1)
        head = cm_width // groups
        for g in range(groups):
            wsg = jnp.where(r >= cc, ws_ref[g], 0.0).astype(BF16)
            bcol = bs_ref[:, g:g + 1]
            for k in range(bm // chunk):
                rows = slice(k * chunk, (k + 1) * chunk)
                cols = slice(g * head, (g + 1) * head)
                mixed = _dot(wsg, vb[rows, cols]) + bcol
                ocm_ref[rows, cols] = (u[rows, cols] * mixed).astype(BF16)
    else:
        ocm_ref[...] = (u * (vn * ws_ref[...] + bs_ref[...])).astype(BF16)


def _in_proj(x, w, *, bm, seq_len, pos0, dims):
    n, d = x.shape
    q_rank, kv_rank, rope_dim, cm_width, chunk, groups = dims
    chunked = seq_len % chunk == 0
    assert chunked or seq_len == 1
    if chunked:
        ws, bs = w["w_s"], w["b_sT"]
    else:
        ws, bs = w["w_s_diag"], w["b_s_diag"]
    kern = functools.partial(_in_proj_kernel, bm=bm, seq_len=seq_len, pos0=pos0, q_rank=q_rank,
                             rope_dim=rope_dim, cm_width=cm_width, chunk=chunk, groups=groups)
    row = lambda width: pl.BlockSpec((bm, width), lambda i: (i, 0))
    consts = [w["g_mix"], w["w_a"], w["w_pe"], w["w_cm"], w["g_q"], w["g_kv"], w["g_v"],
              w["inv128"], w["sign128"], ws, bs]
    out_shape = (
        jax.ShapeDtypeStruct((n, q_rank), BF16),
        jax.ShapeDtypeStruct((n, kv_rank), F32),
        jax.ShapeDtypeStruct((n, rope_dim), F32),
        jax.ShapeDtypeStruct((n, kv_rank), BF16),
        jax.ShapeDtypeStruct((n, rope_dim), BF16),
        jax.ShapeDtypeStruct((n, cm_width), BF16),
        jax.ShapeDtypeStruct((n, cm_width), F32),
        jax.ShapeDtypeStruct((n, LANES), F32),
        jax.ShapeDtypeStruct((n, LANES), F32),
    )
    return pl.pallas_call(
        kern,
        grid=(n // bm,),
        in_specs=[row(d)] + [_const_spec(a.shape) for a in consts],
        out_specs=tuple(row(s.shape[1]) for s in out_shape),
        out_shape=out_shape,
        compiler_params=_params("parallel"),
        name="in_proj",
    )(x, *consts)


def _q_proj_kernel(cq_ref, wuq_ref, wuk_ref, cos_ref, sin_ref, qlat_ref, qpe_ref, *,
                   heads, nope, rope_dim, scale):
    q = _dot(cq_ref[...], wuq_ref[...])
    qn = q[:, :heads * nope].astype(BF16)
    for h in range(heads):
        qlat_ref[h] = (_dot(qn[:, h * nope:(h + 1) * nope], wuk_ref[h]) * scale).astype(BF16)
    qp = q[:, heads * nope:]
    width = heads * rope_dim
    reps = width // LANES
    c = jnp.concatenate([cos_ref[...]] * reps, axis=1)
    s = jnp.concatenate([sin_ref[...]] * reps, axis=1)
    half = rope_dim // 2
    lane = lax.broadcasted_iota(jnp.int32, qp.shape, 1)
    first_half = lax.rem(lane, rope_dim) < half
    partner = jnp.where(first_half, pltpu.roll(qp, width - half, axis=1),
                        pltpu.roll(qp, half, axis=1))
    qr = (qp * c + partner * s) * scale
    for h in range(heads):
        qpe_ref[h] = qr[:, h * rope_dim:(h + 1) * rope_dim].astype(BF16)


def _q_proj(cq, cos, sin, w, *, bm, dims, scale):
    n, q_rank = cq.shape
    heads, nope, rope_dim, kv_rank = dims
    kern = functools.partial(_q_proj_kernel, heads=heads, nope=nope, rope_dim=rope_dim,
                             scale=scale)
    row = lambda width: pl.BlockSpec((bm, width), lambda i: (i, 0))
    return pl.pallas_call(
        kern,
        grid=(n // bm,),
        in_specs=[row(q_rank), _const_spec(w["w_uq"].shape), _const_spec(w["w_ukT"].shape),
                  row(LANES), row(LANES)],
        out_specs=(pl.BlockSpec((heads, bm, kv_rank), lambda i: (0, i, 0)),
                   pl.BlockSpec((heads, bm, rope_dim), lambda i: (0, i, 0))),
        out_shape=(jax.ShapeDtypeStruct((heads, n, kv_rank), BF16),
                   jax.ShapeDtypeStruct((heads, n, rope_dim), BF16)),
        compiler_params=_params("parallel"),
        name="q_proj",
    )(cq, w["w_uq"], w["w_ukT"], cos, sin)


def _attn_prompt_kernel(qlat_ref, qpe_ref, kc_ref, kp_ref, wuv_ref, o_ref,
                        m_ref, l_ref, acc_ref, *, tq, tk, heads):
    qi = pl.program_id(1)
    kv_rank = qlat_ref.shape[-1]
    rope_dim = qpe_ref.shape[-1]
    vdim = wuv_ref.shape[-1]
    rows = heads * tq
    q = qlat_ref[...].reshape(rows, kv_rank)
    qp = qpe_ref[...].reshape(rows, rope_dim)
    m_ref[...] = jnp.full(m_ref.shape, NEG, F32)
    l_ref[...] = jnp.zeros(l_ref.shape, F32)
    acc_ref[...] = jnp.zeros(acc_ref.shape, F32)
    n_full = (qi * tq) // tk

    def step(kb, masked):
        k0 = pl.multiple_of(kb * tk, tk)
        kc = kc_ref[pl.ds(k0, tk), :]
        kp = kp_ref[pl.ds(k0, tk), :]
        s = _dot_nt(q, kc) + _dot_nt(qp, kp)
        if masked:
            qpos = qi * tq + lax.rem(lax.broadcasted_iota(jnp.int32, (rows, tk), 0), tq)
            col = lax.broadcasted_iota(jnp.int32, (rows, tk), 1)
            s = jnp.where(k0 + col <= qpos, s, NEG)
        m_old = m_ref[...]
        m_new = jnp.maximum(m_old, jnp.max(s, axis=1, keepdims=True))
        corr = jnp.exp(m_old - m_new)
        p = jnp.exp(s - jnp.concatenate([m_new] * (tk // LANES), axis=1))
        psum = p[:, :LANES]
        for t in range(1, tk // LANES):
            psum = psum + p[:, t * LANES:(t + 1) * LANES]
        l_ref[...] = l_ref[...] * corr + psum
        acc_ref[...] = (acc_ref[...] * jnp.concatenate([corr] * (kv_rank // LANES), axis=1)
                        + _dot(p.astype(BF16), kc))
        m_ref[...] = m_new

    def body(kb, carry):
        step(kb, False)
        return carry

    lax.fori_loop(0, n_full, body, 0)
    step(n_full, True)
    o = (acc_ref[...] / jnp.sum(l_ref[...], axis=1, keepdims=True)).astype(BF16)
    for h in range(heads):
        o_ref[:, h * vdim:(h + 1) * vdim] = _dot(o[h * tq:(h + 1) * tq], wuv_ref[h]).astype(BF16)


def _attn_prompt(qlat, qpe, kc, kp, wuv, *, batch, seq, tq, tk):
    heads, n, kv_rank = qlat.shape
    rope_dim = qpe.shape[-1]
    vdim = wuv.shape[-1]
    nq = seq // tq
    assert tk % tq == 0 and seq % tk == 0
    kern = functools.partial(_attn_prompt_kernel, tq=tq, tk=tk, heads=heads)
    return pl.pallas_call(
        kern,
        grid=(batch, nq),
        in_specs=[pl.BlockSpec((heads, tq, kv_rank), lambda b, i: (0, b * nq + i, 0)),
                  pl.BlockSpec((heads, tq, rope_dim), lambda b, i: (0, b * nq + i, 0)),
                  pl.BlockSpec((seq, kv_rank), lambda b, i: (b, 0)),
                  pl.BlockSpec((seq, rope_dim), lambda b, i: (b, 0)),
                  _const_spec(wuv.shape)],
        out_specs=pl.BlockSpec((tq, heads * vdim), lambda b, i: (b * nq + i, 0)),
        out_shape=jax.ShapeDtypeStruct((n, heads * vdim), BF16),
        scratch_shapes=[pltpu.VMEM((heads * tq, LANES), F32), pltpu.VMEM((heads * tq, LANES), F32),
                        pltpu.VMEM((heads * tq, kv_rank), F32)],
        compiler_params=_params("parallel", "arbitrary"),
        name="attn_prompt",
    )(qlat, qpe, kc, kp, wuv)


def _attn_sample_kernel(pt_ref, qlat_ref, qpe_ref, cnew_ref, knew_ref, ckv_hbm, kpe_hbm, o_ref,
                        kbuf, pbuf, sem_k, sem_p, *, cpp, nbuf, n_chunks):
    b = pl.program_id(0)
    nb = pl.num_programs(0)
    total = nb * n_chunks
    page, kv_rank = kbuf.shape[2], kbuf.shape[3]
    ahead = nbuf - 1

    def copies(g, slot):
        out = []
        for j in range(cpp):
            pid = pt_ref[g * cpp + j]
            out.append(pltpu.make_async_copy(ckv_hbm.at[pid], kbuf.at[slot, j], sem_k.at[slot, j]))
            out.append(pltpu.make_async_copy(kpe_hbm.at[pid], pbuf.at[slot, j], sem_p.at[slot, j]))
        return out

    @pl.when(b == 0)
    def _():
        for g in range(ahead):
            for cp in copies(g, g % nbuf):
                cp.start()

    q = qlat_ref[...]
    qp = qpe_ref[...]
    heads = q.shape[0]
    m = jnp.full((heads, 1), NEG, F32)
    l = jnp.zeros((heads, 1), F32)
    acc = jnp.zeros((heads, kv_rank), F32)
    for c in range(n_chunks):
        g = b * n_chunks + c
        slot = c % nbuf
        for cp in copies(g, slot):
            cp.wait()
        for cp in copies(jnp.minimum(g + ahead, total - 1), (c + ahead) % nbuf):
            cp.start()
        kc = kbuf[slot].reshape(cpp * page, kv_rank).astype(BF16)
        kp = jnp.concatenate([pbuf[slot, j] for j in range(cpp)], axis=1).astype(BF16)
        s = _dot_nt(q, kc) + _dot(qp, kp)
        m_new = jnp.maximum(m, jnp.max(s, axis=1, keepdims=True))
        corr = jnp.exp(m - m_new)
        p = jnp.exp(s - m_new)
        l = l * corr + jnp.sum(p, axis=1, keepdims=True)
        acc = acc * corr + _dot(p.astype(BF16), kc)
        m = m_new

    @pl.when(b == nb - 1)
    def _():
        for k in range(ahead):
            for cp in copies(total - 1, (n_chunks + k) % nbuf):
                cp.wait()

    cn = cnew_ref[...]
    s_new = (jnp.sum(q.astype(F32) * cn, axis=1, keepdims=True)
             + jnp.sum(qp.astype(F32) * knew_ref[...], axis=1, keepdims=True))
    m_tot = jnp.maximum(m, s_new)
    c_old = jnp.exp(m - m_tot)
    p_new = jnp.exp(s_new - m_tot)
    o_ref[...] = (acc * c_old + p_new * cn) / (l * c_old + p_new)


def _attn_sample(qlat, qpe, cnew, knew, cache_ckv, cache_kpe_t, page_table, *, cpp, nbuf):
    bd, heads, kv_rank = qlat.shape
    rope_dim = qpe.shape[-1]
    n_pages = page_table.shape[1]
    page = cache_ckv.shape[1]
    n_chunks = n_pages // cpp
    assert n_pages % cpp == 0 and n_chunks % nbuf == 0 and nbuf >= 2
    pt = page_table.reshape(-1)
    kern = functools.partial(_attn_sample_kernel, cpp=cpp, nbuf=nbuf, n_chunks=n_chunks)
    per_b = lambda r, width: pl.BlockSpec((None, r, width), lambda b, pt_ref: (b, 0, 0))
    grid_spec = pltpu.PrefetchScalarGridSpec(
        num_scalar_prefetch=1,
        grid=(bd,),
        in_specs=[per_b(heads, kv_rank), per_b(heads, rope_dim), per_b(1, kv_rank),
                  per_b(1, rope_dim), pl.BlockSpec(memory_space=pl.ANY),
                  pl.BlockSpec(memory_space=pl.ANY)],
        out_specs=per_b(heads, kv_rank),
        scratch_shapes=[pltpu.VMEM((nbuf, cpp, page, kv_rank), F32),
                        pltpu.VMEM((nbuf, cpp, rope_dim, page), F32),
                        pltpu.SemaphoreType.DMA((nbuf, cpp)), pltpu.SemaphoreType.DMA((nbuf, cpp))],
    )
    return pl.pallas_call(
        kern,
        grid_spec=grid_spec,
        out_shape=jax.ShapeDtypeStruct((bd, heads, kv_rank), F32),
        compiler_params=_params("arbitrary"),
        name="attn_sample",
    )(pt, qlat, qpe, cnew, knew, cache_ckv, cache_kpe_t)


def _uv_kernel(o_ref, wuv_ref, out_ref, *, heads):
    kv_rank = wuv_ref.shape[1]
    vdim = wuv_ref.shape[2]
    for h in range(heads):
        oh = o_ref[:, h * kv_rank:(h + 1) * kv_rank].astype(BF16)
        out_ref[:, h * vdim:(h + 1) * vdim] = _dot(oh, wuv_ref[h]).astype(BF16)


def _uv_proj(o_lat, wuv):
    n = o_lat.shape[0]
    heads, kv_rank, vdim = wuv.shape
    return pl.pallas_call(
        functools.partial(_uv_kernel, heads=heads),
        grid=(1,),
        in_specs=[_const_spec(o_lat.shape), _const_spec(wuv.shape)],
        out_specs=_const_spec((n, heads * vdim)),
        out_shape=jax.ShapeDtypeStruct((n, heads * vdim), BF16),
        compiler_params=_params("arbitrary"),
        name="uv_proj",
    )(o_lat, wuv)


def _out_proj_kernel(x_ref, oa_ref, ocm_ref, wo_ref, gffn_ref, h_ref, hf_ref):
    wa = oa_ref.shape[1]
    h1 = x_ref[...] + _dot(oa_ref[...], wo_ref[:wa, :]) + _dot(ocm_ref[...], wo_ref[wa:, :])
    h_ref[...] = h1
    hf_ref[...] = _rms(h1, gffn_ref[...]).astype(BF16)


def _out_proj(x, oa, ocm, w, *, bm):
    n, d = x.shape
    row = lambda width: pl.BlockSpec((bm, width), lambda i: (i, 0))
    return pl.pallas_call(
        _out_proj_kernel,
        grid=(n // bm,),
        in_specs=[row(d), row(oa.shape[1]), row(ocm.shape[1]), _const_spec(w["w_o"].shape),
                  _const_spec(w["g_ffn"].shape)],
        out_specs=(row(d), row(d)),
        out_shape=(jax.ShapeDtypeStruct((n, d), F32), jax.ShapeDtypeStruct((n, d), BF16)),
        compiler_params=_params("parallel"),
        name="out_proj",
    )(x, oa, ocm, w["w_o"], w["g_ffn"])


def _ffn_kernel(h_ref, hf_ref, wg_ref, wu_ref, wd_ref, out_ref):
    j = pl.program_id(1)

    @pl.when(j == 0)
    def _():
        out_ref[...] = h_ref[...]

    hf = hf_ref[...]
    t = (jax.nn.silu(_dot(hf, wg_ref[...])) * _dot(hf, wu_ref[...])).astype(BF16)
    out_ref[...] += _dot(t, wd_ref[...])


def _ffn(h, hf, w, *, bm, bh):
    n, d = h.shape
    hidden = w["w_gate"].shape[1]
    assert hidden % bh == 0
    return pl.pallas_call(
        _ffn_kernel,
        grid=(n // bm, hidden // bh),
        in_specs=[pl.BlockSpec((bm, d), lambda i, j: (i, 0)),
                  pl.BlockSpec((bm, d), lambda i, j: (i, 0)),
                  pl.BlockSpec((d, bh), lambda i, j: (0, j)),
                  pl.BlockSpec((d, bh), lambda i, j: (0, j)),
                  pl.BlockSpec((bh, d), lambda i, j: (j, 0))],
        out_specs=pl.BlockSpec((bm, d), lambda i, j: (i, 0)),
        out_shape=jax.ShapeDtypeStruct((n, d), F32),
        compiler_params=_params("parallel", "arbitrary"),
        name="ffn",
    )(h, hf, w["w_gate"], w["w_up"], w["w_down"])


def _ple_kernel(h_ref, p_ref, gple_ref, wg_ref, wp_ref, gfin_ref, y_ref, *, final):
    h = h_ref[...]
    gate = jax.nn.sigmoid(_dot(_rms(h, gple_ref[...]).astype(BF16), wg_ref[...]))
    h3 = h + gate * _dot(p_ref[...].astype(BF16), wp_ref[...])
    y_ref[...] = _rms(h3, gfin_ref[...]) if final else h3


def _ple(h, p, w, g_final, *, bm, final):
    n, d = h.shape
    row = lambda width: pl.BlockSpec((bm, width), lambda i: (i, 0))
    return pl.pallas_call(
        functools.partial(_ple_kernel, final=final),
        grid=(n // bm,),
        in_specs=[row(d), row(p.shape[1]), _const_spec(w["g_ple"].shape),
                  _const_spec(w["w_ple_gate"].shape), _const_spec(w["w_ple_proj"].shape),
                  _const_spec(g_final.shape)],
        out_specs=row(d),
        out_shape=jax.ShapeDtypeStruct((n, d), F32),
        compiler_params=_params("parallel"),
        name="ple",
    )(h, p, w["g_ple"], w["w_ple_gate"], w["w_ple_proj"], g_final)


def _prep_weights(i, g_mix, w_in, g_q, w_uq, g_kv, w_uk, w_uv, g_v, w_s, b_s, w_o, g_ffn,
                  w_ffn_gate, w_ffn_up, w_ffn_down, g_ple, w_ple_gate, w_ple_proj):
    q_rank = g_q.shape[1]
    kv_rank = g_kv.shape[1]
    heads, qk_dim = w_uq.shape[2], w_uq.shape[3]
    nope = w_uk.shape[3]
    rope_dim = qk_dim - nope
    groups = w_s.shape[1]
    cm_width = g_v.shape[1]
    wi = w_in[i]
    pe = wi[:, q_rank + kv_rank:q_rank + kv_rank + rope_dim]
    half = rope_dim // 2
    inv = ROPE_THETA ** (-jnp.arange(half, dtype=F32) / half)
    lane = jnp.arange(LANES)
    return {
        "g_mix": g_mix[i][None], "g_q": g_q[i][None], "g_kv": g_kv[i][None], "g_v": g_v[i][None],
        "g_ffn": g_ffn[i][None], "g_ple": g_ple[i][None],
        "w_a": wi[:, :q_rank + kv_rank].astype(BF16),
        "w_pe": jnp.concatenate([pe] * (LANES // rope_dim), axis=1).astype(BF16),
        "w_cm": wi[:, q_rank + kv_rank + rope_dim:].astype(BF16),
        "inv128": jnp.tile(inv, LANES // half)[None],
        "sign128": jnp.where((lane % rope_dim) < half, -1.0, 1.0).astype(F32)[None],
        "w_s": w_s[i], "b_sT": b_s[i].T,
        "w_s_diag": jnp.repeat(w_s[i][:, 0, 0], cm_width // groups)[None],
        "b_s_diag": jnp.repeat(b_s[i][:, 0], cm_width // groups)[None],
        "w_uq": jnp.concatenate([w_uq[i][:, :, :nope].reshape(q_rank, heads * nope),
                                 w_uq[i][:, :, nope:].reshape(q_rank, heads * rope_dim)],
                                axis=1).astype(BF16),
        "w_ukT": jnp.transpose(w_uk[i], (1, 2, 0)).astype(BF16),
        "w_uvT": jnp.transpose(w_uv[i], (1, 0, 2)).astype(BF16),
        "w_o": w_o[i].astype(BF16),
        "w_gate": w_ffn_gate[i].astype(BF16), "w_up": w_ffn_up[i].astype(BF16),
        "w_down": w_ffn_down[i].astype(BF16),
        "w_ple_gate": w_ple_gate[i].astype(BF16), "w_ple_proj": w_ple_proj[i].astype(BF16),
    }


def kernel(x_prompt, x_sample, p_prompt, p_sample, cache_ckv, cache_kpe, page_table, g_mix, w_in,
           g_q, w_uq, g_kv, w_uk, w_uv, g_v, w_s, b_s, w_o, g_ffn, w_ffn_gate, w_ffn_up,
           w_ffn_down, g_ple, w_ple_gate, w_ple_proj, g_final):
    batch, seq, d_model = x_prompt.shape
    dec_batch, dec_seq, _ = x_sample.shape
    depth = g_mix.shape[0]
    q_rank, kv_rank = g_q.shape[1], g_kv.shape[1]
    heads, qk_dim = w_uq.shape[2], w_uq.shape[3]
    nope = w_uk.shape[3]
    rope_dim = qk_dim - nope
    cm_width = g_v.shape[1]
    groups, chunk = w_s.shape[1], w_s.shape[2]
    page = cache_ckv.shape[2]
    past_len = page_table.shape[1] * page
    scale = 1.0 / math.sqrt(nope + rope_dim)
    assert dec_seq == 1 and seq % chunk == 0
    in_dims = (q_rank, kv_rank, rope_dim, cm_width, chunk, groups)
    q_dims = (heads, nope, rope_dim, kv_rank)
    gfin = g_final[None]

    hp = x_prompt.reshape(batch * seq, d_model)
    hs = x_sample.reshape(dec_batch * dec_seq, d_model)
    outs = [[] for _ in range(6)]
    for i in range(depth):
        w = _prep_weights(i, g_mix, w_in, g_q, w_uq, g_kv, w_uk, w_uv, g_v, w_s, b_s, w_o, g_ffn,
                          w_ffn_gate, w_ffn_up, w_ffn_down, g_ple, w_ple_gate, w_ple_proj)

        cq, ckv, kpe, ckvb, kpeb, ocm, vn, cos, sin = _in_proj(
            hp, w, bm=512, seq_len=seq, pos0=0, dims=in_dims)
        qlat, qpe = _q_proj(cq, cos, sin, w, bm=512, dims=q_dims, scale=scale)
        oa = _attn_prompt(qlat, qpe, ckvb, kpeb, w["w_uvT"], batch=batch, seq=seq, tq=128,
                          tk=512)
        h1, hf = _out_proj(hp, oa, ocm, w, bm=512)
        h2 = _ffn(h1, hf, w, bm=512, bh=512)
        hp = _ple(h2, p_prompt[i].reshape(batch * seq, -1), w, gfin, bm=512,
                  final=i == depth - 1)
        outs[0].append(ckv.reshape(batch, seq, kv_rank))
        outs[1].append(kpe.reshape(batch, seq, rope_dim))
        outs[2].append(vn.reshape(batch, seq, cm_width)[:, seq - chunk:])

        cq, ckv, kpe, ckvb, kpeb, ocm, vn, cos, sin = _in_proj(
            hs, w, bm=dec_batch, seq_len=dec_seq, pos0=past_len, dims=in_dims)
        qlat, qpe = _q_proj(cq, cos, sin, w, bm=dec_batch, dims=q_dims, scale=scale)
        o_lat = _attn_sample(jnp.transpose(qlat, (1, 0, 2)), jnp.transpose(qpe, (1, 0, 2)),
                             ckv[:, None, :], kpe[:, None, :], cache_ckv[i],
                             jnp.swapaxes(cache_kpe[i], 1, 2), page_table, cpp=16, nbuf=4)
        oa = _uv_proj(o_lat.reshape(dec_batch, heads * kv_rank), w["w_uvT"])
        h1, hf = _out_proj(hs, oa, ocm, w, bm=dec_batch)
        h2 = _ffn(h1, hf, w, bm=dec_batch, bh=512)
        hs = _ple(h2, p_sample[i].reshape(dec_batch * dec_seq, -1), w, gfin, bm=dec_batch,
                  final=i == depth - 1)
        outs[3].append(ckv.reshape(dec_batch, dec_seq, kv_rank))
        outs[4].append(kpe.reshape(dec_batch, dec_seq, rope_dim))
        outs[5].append(vn.reshape(dec_batch, dec_seq, cm_width))

    y_prompt = hp.reshape(batch, seq, d_model)
    y_sample = hs.reshape(dec_batch, dec_seq, d_model)
    return (y_prompt, y_sample) + tuple(jnp.stack(o) for o in outs)
```

```python
import functools
import math

import jax
import jax.numpy as jnp
from jax import lax
from jax.experimental import pallas as pl
from jax.experimental.pallas import tpu as pltpu

F32 = jnp.float32
BF16 = jnp.bfloat16

EPS = 1e-6
NEG = -1e30
ROPE_THETA = 10000.0
LANES = 128
SUBLANES = 8
VMEM_LIMIT = 56 * 1024 * 1024


def _params(*sem):
    return pltpu.CompilerParams(dimension_semantics=sem, vmem_limit_bytes=VMEM_LIMIT)


def _rms(x, g):
    return x * lax.rsqrt(jnp.mean(x * x, axis=-1, keepdims=True) + EPS) * g


def _dot(a, b):
    return jnp.dot(a, b, preferred_element_type=F32)


def _dot_nt(a, b):
    return lax.dot_general(a, b, (((1,), (1,)), ((), ())), preferred_element_type=F32)


def _const_spec(shape):
    nd = len(shape)
    return pl.BlockSpec(shape, lambda *_: (0,) * nd)


def _in_proj_kernel(x_ref, gmix_ref, wa_ref, wpe_ref, wcm_ref, gq_ref, gkv_ref, gv_ref,
                    inv_ref, sign_ref, ws_ref, bs_ref,
                    cq_ref, ckv_ref, kpe_ref, ckvb_ref, kpeb_ref, ocm_ref, vn_ref,
                    cos_ref, sin_ref, *, bm, seq_len, pos0, q_rank, rope_dim, cm_width,
                    chunk, groups):
    i = pl.program_id(0)
    hn = _rms(x_ref[...], gmix_ref[...]).astype(BF16)

    za = _dot(hn, wa_ref[...])
    cq_ref[...] = _rms(za[:, :q_rank], gq_ref[...]).astype(BF16)
    ckv = _rms(za[:, q_rank:], gkv_ref[...])
    ckv_ref[...] = ckv
    ckvb_ref[...] = ckv.astype(BF16)

    row = i * bm + lax.broadcasted_iota(jnp.int32, (bm, LANES), 0)
    pos = pos0 + lax.rem(row, seq_len)
    ang = pos.astype(F32) * inv_ref[...]
    c = jnp.cos(ang)
    s = jnp.sin(ang) * sign_ref[...]
    cos_ref[...] = c
    sin_ref[...] = s
    zpe = _dot(hn, wpe_ref[...])
    kpe = (zpe * c + pltpu.roll(zpe, rope_dim // 2, axis=1) * s)[:, :rope_dim]
    kpe_ref[...] = kpe
    kpeb_ref[...] = kpe.astype(BF16)

    gz = jax.nn.gelu(_dot(hn, wcm_ref[...]))
    u = gz[:, :cm_width]
    vn = _rms(gz[:, cm_width:], gv_ref[...])
    vn_ref[...] = vn
    if seq_len % chunk == 0:
        vb = vn.astype(BF16)
        r = lax.broadcasted_iota(jnp.int32, (chunk, chunk), 0)
        cc = lax.broadcasted_iota(jnp.int32, (chunk, chunk), 1)
        head = cm_width // groups
        for g in range(groups):
            wsg = jnp.where(r >= cc, ws_ref[g], 0.0).astype(BF16)
            bcol = bs_ref[:, g:g + 1]
            for k in range(bm // chunk):
                rows = slice(k * chunk, (k + 1) * chunk)
                cols = slice(g * head, (g + 1) * head)
                mixed = _dot(wsg, vb[rows, cols]) + bcol
                ocm_ref[rows, cols] = (u[rows, cols] * mixed).astype(BF16)
    else:
        ocm_ref[...] = (u * (vn * ws_ref[...] + bs_ref[...])).astype(BF16)


def _in_proj(x, w, *, bm, seq_len, pos0, dims):
    n, d = x.shape
    q_rank, kv_rank, rope_dim, cm_width, chunk, groups = dims
    chunked = seq_len % chunk == 0
    assert chunked or seq_len == 1
    if chunked:
        ws, bs = w["w_s"], w["b_sT"]
    else:
        ws, bs = w["w_s_diag"], w["b_s_diag"]
    kern = functools.partial(_in_proj_kernel, bm=bm, seq_len=seq_len, pos0=pos0, q_rank=q_rank,
                             rope_dim=rope_dim, cm_width=cm_width, chunk=chunk, groups=groups)
    row = lambda width: pl.BlockSpec((bm, width), lambda i: (i, 0))
    consts = [w["g_mix"], w["w_a"], w["w_pe"], w["w_cm"], w["g_q"], w["g_kv"], w["g_v"],
              w["inv128"], w["sign128"], ws, bs]
    out_shape = (
        jax.ShapeDtypeStruct((n, q_rank), BF16),
        jax.ShapeDtypeStruct((n, kv_rank), F32),
        jax.ShapeDtypeStruct((n, rope_dim), F32),
        jax.ShapeDtypeStruct((n, kv_rank), BF16),
        jax.ShapeDtypeStruct((n, rope_dim), BF16),
        jax.ShapeDtypeStruct((n, cm_width), BF16),
        jax.ShapeDtypeStruct((n, cm_width), F32),
        jax.ShapeDtypeStruct((n, LANES), F32),
        jax.ShapeDtypeStruct((n, LANES), F32),
    )
    return pl.pallas_call(
        kern,
        grid=(n // bm,),
        in_specs=[row(d)] + [_const_spec(a.shape) for a in consts],
        out_specs=tuple(row(s.shape[1]) for s in out_shape),
        out_shape=out_shape,
        compiler_params=_params("parallel"),
        name="in_proj",
    )(x, *consts)


def _q_proj_kernel(cq_ref, wuq_ref, wuk_ref, cos_ref, sin_ref, qlat_ref, qpe_ref, *,
                   heads, nope, rope_dim, scale):
    q = _dot(cq_ref[...], wuq_ref[...])
    qn = q[:, :heads * nope].astype(BF16)
    for h in range(heads):
        qlat_ref[h] = (_dot(qn[:, h * nope:(h + 1) * nope], wuk_ref[h]) * scale).astype(BF16)
    qp = q[:, heads * nope:]
    width = heads * rope_dim
    reps = width // LANES
    c = jnp.concatenate([cos_ref[...]] * reps, axis=1)
    s = jnp.concatenate([sin_ref[...]] * reps, axis=1)
    half = rope_dim // 2
    lane = lax.broadcasted_iota(jnp.int32, qp.shape, 1)
    first_half = lax.rem(lane, rope_dim) < half
    partner = jnp.where(first_half, pltpu.roll(qp, width - half, axis=1),
                        pltpu.roll(qp, half, axis=1))
    qr = (qp * c + partner * s) * scale
    for h in range(heads):
        qpe_ref[h] = qr[:, h * rope_dim:(h + 1) * rope_dim].astype(BF16)


def _q_proj(cq, cos, sin, w, *, bm, dims, scale):
    n, q_rank = cq.shape
    heads, nope, rope_dim, kv_rank = dims
    kern = functools.partial(_q_proj_kernel, heads=heads, nope=nope, rope_dim=rope_dim,
                             scale=scale)
    row = lambda width: pl.BlockSpec((bm, width), lambda i: (i, 0))
    return pl.pallas_call(
        kern,
        grid=(n // bm,),
        in_specs=[row(q_rank), _const_spec(w["w_uq"].shape), _const_spec(w["w_ukT"].shape),
                  row(LANES), row(LANES)],
        out_specs=(pl.BlockSpec((heads, bm, kv_rank), lambda i: (0, i, 0)),
                   pl.BlockSpec((heads, bm, rope_dim), lambda i: (0, i, 0))),
        out_shape=(jax.ShapeDtypeStruct((heads, n, kv_rank), BF16),
                   jax.ShapeDtypeStruct((heads, n, rope_dim), BF16)),
        compiler_params=_params("parallel"),
        name="q_proj",
    )(cq, w["w_uq"], w["w_ukT"], cos, sin)


def _attn_prompt_kernel(qlat_ref, qpe_ref, kc_ref, kp_ref, wuv_ref, *rest, tq, tk, heads, n_cast):
    cast_in = rest[:n_cast]
    o_ref = rest[n_cast]
    cast_out = rest[n_cast + 1:2 * n_cast + 1]
    m_ref, l_ref, acc_ref = rest[2 * n_cast + 1:]
    for src, dst in zip(cast_in, cast_out):
        dst[...] = src[...].astype(BF16)
    qi = pl.program_id(1)
    kv_rank = qlat_ref.shape[-1]
    rope_dim = qpe_ref.shape[-1]
    vdim = wuv_ref.shape[-1]
    rows = heads * tq
    q = qlat_ref[...].reshape(rows, kv_rank)
    qp = qpe_ref[...].reshape(rows, rope_dim)
    m_ref[...] = jnp.full(m_ref.shape, NEG, F32)
    l_ref[...] = jnp.zeros(l_ref.shape, F32)
    acc_ref[...] = jnp.zeros(acc_ref.shape, F32)
    n_full = (qi * tq) // tk

    def step(kb, masked):
        k0 = pl.multiple_of(kb * tk, tk)
        kc = kc_ref[pl.ds(k0, tk), :]
        kp = kp_ref[pl.ds(k0, tk), :]
        s = _dot_nt(q, kc) + _dot_nt(qp, kp)
        if masked:
            qpos = qi * tq + lax.rem(lax.broadcasted_iota(jnp.int32, (rows, tk), 0), tq)
            col = lax.broadcasted_iota(jnp.int32, (rows, tk), 1)
            s = jnp.where(k0 + col <= qpos, s, NEG)
        m_old = m_ref[...]
        m_new = jnp.maximum(m_old, jnp.max(s, axis=1, keepdims=True))
        corr = jnp.exp(m_old - m_new)
        p = jnp.exp(s - jnp.concatenate([m_new] * (tk // LANES), axis=1))
        psum = p[:, :LANES]
        for t in range(1, tk // LANES):
            psum = psum + p[:, t * LANES:(t + 1) * LANES]
        l_ref[...] = l_ref[...] * corr + psum
        acc_ref[...] = (acc_ref[...] * jnp.concatenate([corr] * (kv_rank // LANES), axis=1)
                        + _dot(p.astype(BF16), kc))
        m_ref[...] = m_new

    def body(kb, carry):
        step(kb, False)
        return carry

    lax.fori_loop(0, n_full, body, 0)
    step(n_full, True)
    o = (acc_ref[...] / jnp.sum(l_ref[...], axis=1, keepdims=True)).astype(BF16)
    for h in range(heads):
        o_ref[:, h * vdim:(h + 1) * vdim] = _dot(o[h * tq:(h + 1) * tq], wuv_ref[h]).astype(BF16)


BF16_SUBLANES = 16


def _slab_spec(arr, steps, nq):
    r, c = arr.shape
    n_slabs = steps
    while r % n_slabs or (r // n_slabs) % BF16_SUBLANES:
        n_slabs //= 2
    assert n_slabs >= 1
    return pl.BlockSpec((r // n_slabs, c), lambda b, i: (jnp.minimum(b * nq + i, n_slabs - 1), 0))


def _attn_prompt(qlat, qpe, kc, kp, wuv, to_cast, *, batch, seq, tq, tk):
    heads, n, kv_rank = qlat.shape
    rope_dim = qpe.shape[-1]
    vdim = wuv.shape[-1]
    nq = seq // tq
    assert tk % tq == 0 and seq % tk == 0
    kern = functools.partial(_attn_prompt_kernel, tq=tq, tk=tk, heads=heads, n_cast=len(to_cast))
    slabs = [_slab_spec(a, batch * nq, nq) for a in to_cast]
    outs = pl.pallas_call(
        kern,
        grid=(batch, nq),
        in_specs=[pl.BlockSpec((heads, tq, kv_rank), lambda b, i: (0, b * nq + i, 0)),
                  pl.BlockSpec((heads, tq, rope_dim), lambda b, i: (0, b * nq + i, 0)),
                  pl.BlockSpec((seq, kv_rank), lambda b, i: (b, 0)),
                  pl.BlockSpec((seq, rope_dim), lambda b, i: (b, 0)),
                  _const_spec(wuv.shape)] + slabs,
        out_specs=[pl.BlockSpec((tq, heads * vdim), lambda b, i: (b * nq + i, 0))] + slabs,
        out_shape=[jax.ShapeDtypeStruct((n, heads * vdim), BF16)]
                  + [jax.ShapeDtypeStruct(a.shape, BF16) for a in to_cast],
        scratch_shapes=[pltpu.VMEM((heads * tq, LANES), F32), pltpu.VMEM((heads * tq, LANES), F32),
                        pltpu.VMEM((heads * tq, kv_rank), F32)],
        compiler_params=_params("parallel", "arbitrary"),
        name="attn_prompt",
    )(qlat, qpe, kc, kp, wuv, *to_cast)
    return outs[0], outs[1:]


def _attn_sample_kernel(pt_ref, qlat_ref, qpe_ref, cnew_ref, knew_ref, ckv_hbm, kpe_hbm, o_ref,
                        kbuf, pbuf, sem_k, sem_p, *, cpp, nbuf, n_chunks):
    b = pl.program_id(0)
    nb = pl.num_programs(0)
    total = nb * n_chunks
    page, kv_rank = kbuf.shape[2], kbuf.shape[3]
    ahead = nbuf - 2

    def copies(g, slot):
        out = []
        for j in range(cpp):
            pid = pt_ref[g * cpp + j]
            out.append(pltpu.make_async_copy(ckv_hbm.at[pid], kbuf.at[slot, j], sem_k.at[slot, j]))
            out.append(pltpu.make_async_copy(kpe_hbm.at[pid], pbuf.at[slot, j], sem_p.at[slot, j]))
        return out

    @pl.when(b == 0)
    def _():
        for g in range(ahead):
            for cp in copies(g, g % nbuf):
                cp.start()

    q = qlat_ref[...]
    qp = qpe_ref[...]
    heads = q.shape[0]
    m = jnp.full((heads, 1), NEG, F32)
    l = jnp.zeros((heads, 1), F32)
    acc = jnp.zeros((heads, kv_rank), F32)

    def fetch(c):
        g = b * n_chunks + c
        slot = c % nbuf
        for cp in copies(g, slot):
            cp.wait()
        for cp in copies(jnp.minimum(g + ahead, total - 1), (c + ahead) % nbuf):
            cp.start()
        kc = kbuf[slot].reshape(cpp * page, kv_rank).astype(BF16)
        kp = jnp.concatenate([pbuf[slot, j] for j in range(cpp)], axis=1).astype(BF16)
        return kc, _dot_nt(q, kc) + _dot(qp, kp)

    kc, s = fetch(0)
    for c in range(n_chunks):
        if c + 1 < n_chunks:
            kc_next, s_next = fetch(c + 1)
        m_new = jnp.maximum(m, jnp.max(s, axis=1, keepdims=True))
        corr = jnp.exp(m - m_new)
        p = jnp.exp(s - m_new)
        l = l * corr + jnp.sum(p, axis=1, keepdims=True)
        acc = acc * corr + _dot(p.astype(BF16), kc)
        m = m_new
        if c + 1 < n_chunks:
            kc, s = kc_next, s_next

    @pl.when(b == nb - 1)
    def _():
        for k in range(ahead):
            for cp in copies(total - 1, (n_chunks + k) % nbuf):
                cp.wait()

    cn = cnew_ref[...]
    s_new = (jnp.sum(q.astype(F32) * cn, axis=1, keepdims=True)
             + jnp.sum(qp.astype(F32) * knew_ref[...], axis=1, keepdims=True))
    m_tot = jnp.maximum(m, s_new)
    c_old = jnp.exp(m - m_tot)
    p_new = jnp.exp(s_new - m_tot)
    o_ref[...] = (acc * c_old + p_new * cn) / (l * c_old + p_new)


def _attn_sample(qlat, qpe, cnew, knew, cache_ckv, cache_kpe_t, page_table, *, cpp, nbuf):
    bd, heads, kv_rank = qlat.shape
    rope_dim = qpe.shape[-1]
    n_pages = page_table.shape[1]
    page = cache_ckv.shape[1]
    n_chunks = n_pages // cpp
    assert n_pages % cpp == 0 and n_chunks % nbuf == 0 and nbuf >= 3
    pt = page_table.reshape(-1)
    kern = functools.partial(_attn_sample_kernel, cpp=cpp, nbuf=nbuf, n_chunks=n_chunks)
    per_b = lambda r, width: pl.BlockSpec((None, r, width), lambda b, pt_ref: (b, 0, 0))
    grid_spec = pltpu.PrefetchScalarGridSpec(
        num_scalar_prefetch=1,
        grid=(bd,),
        in_specs=[per_b(heads, kv_rank), per_b(heads, rope_dim), per_b(1, kv_rank),
                  per_b(1, rope_dim), pl.BlockSpec(memory_space=pl.ANY),
                  pl.BlockSpec(memory_space=pl.ANY)],
        out_specs=per_b(heads, kv_rank),
        scratch_shapes=[pltpu.VMEM((nbuf, cpp, page, kv_rank), F32),
                        pltpu.VMEM((nbuf, cpp, rope_dim, page), F32),
                        pltpu.SemaphoreType.DMA((nbuf, cpp)), pltpu.SemaphoreType.DMA((nbuf, cpp))],
    )
    return pl.pallas_call(
        kern,
        grid_spec=grid_spec,
        out_shape=jax.ShapeDtypeStruct((bd, heads, kv_rank), F32),
        compiler_params=_params("arbitrary"),
        name="attn_sample",
    )(pt, qlat, qpe, cnew, knew, cache_ckv, cache_kpe_t)


def _uv_kernel(o_ref, wuv_ref, out_ref, *, heads):
    kv_rank = wuv_ref.shape[1]
    vdim = wuv_ref.shape[2]
    for h in range(heads):
        oh = o_ref[:, h * kv_rank:(h + 1) * kv_rank].astype(BF16)
        out_ref[:, h * vdim:(h + 1) * vdim] = _dot(oh, wuv_ref[h]).astype(BF16)


def _uv_proj(o_lat, wuv):
    n = o_lat.shape[0]
    heads, kv_rank, vdim = wuv.shape
    return pl.pallas_call(
        functools.partial(_uv_kernel, heads=heads),
        grid=(1,),
        in_specs=[_const_spec(o_lat.shape), _const_spec(wuv.shape)],
        out_specs=_const_spec((n, heads * vdim)),
        out_shape=jax.ShapeDtypeStruct((n, heads * vdim), BF16),
        compiler_params=_params("arbitrary"),
        name="uv_proj",
    )(o_lat, wuv)


def _out_proj_kernel(x_ref, oa_ref, ocm_ref, wo_ref, gffn_ref, h_ref, hf_ref):
    wa = oa_ref.shape[1]
    h1 = x_ref[...] + _dot(oa_ref[...], wo_ref[:wa, :]) + _dot(ocm_ref[...], wo_ref[wa:, :])
    h_ref[...] = h1
    hf_ref[...] = _rms(h1, gffn_ref[...]).astype(BF16)


def _out_proj(x, oa, ocm, w, *, bm):
    n, d = x.shape
    row = lambda width: pl.BlockSpec((bm, width), lambda i: (i, 0))
    return pl.pallas_call(
        _out_proj_kernel,
        grid=(n // bm,),
        in_specs=[row(d), row(oa.shape[1]), row(ocm.shape[1]), _const_spec(w["w_o"].shape),
                  _const_spec(w["g_ffn"].shape)],
        out_specs=(row(d), row(d)),
        out_shape=(jax.ShapeDtypeStruct((n, d), F32), jax.ShapeDtypeStruct((n, d), BF16)),
        compiler_params=_params("parallel"),
        name="out_proj",
    )(x, oa, ocm, w["w_o"], w["g_ffn"])


def _ffn_kernel(hf_ref, wg_ref, wu_ref, wd_ref, out_ref):
    j = pl.program_id(1)

    @pl.when(j == 0)
    def _():
        out_ref[...] = jnp.zeros(out_ref.shape, F32)

    hf = hf_ref[...]
    t = (jax.nn.silu(_dot(hf, wg_ref[...])) * _dot(hf, wu_ref[...])).astype(BF16)
    out_ref[...] += _dot(t, wd_ref[...])


def _ffn(hf, w, *, bm, bh):
    n, d = hf.shape
    hidden = w["w_gate"].shape[1]
    assert hidden % bh == 0 and n % bm == 0
    return pl.pallas_call(
        _ffn_kernel,
        grid=(n // bm, hidden // bh),
        in_specs=[pl.BlockSpec((bm, d), lambda i, j: (i, 0)),
                  pl.BlockSpec((d, bh), lambda i, j: (0, j)),
                  pl.BlockSpec((d, bh), lambda i, j: (0, j)),
                  pl.BlockSpec((bh, d), lambda i, j: (j, 0))],
        out_specs=pl.BlockSpec((bm, d), lambda i, j: (i, 0)),
        out_shape=jax.ShapeDtypeStruct((n, d), F32),
        compiler_params=_params("parallel", "arbitrary"),
        name="ffn",
    )(hf, w["w_gate"], w["w_up"], w["w_down"])


def _ple_kernel(h_ref, f_ref, p_ref, gple_ref, wg_ref, wp_ref, gfin_ref, y_ref, *, final):
    h = h_ref[...] + f_ref[...]
    gate = jax.nn.sigmoid(_dot(_rms(h, gple_ref[...]).astype(BF16), wg_ref[...]))
    h3 = h + gate * _dot(p_ref[...].astype(BF16), wp_ref[...])
    y_ref[...] = _rms(h3, gfin_ref[...]) if final else h3


def _ple(h, f, p, w, g_final, *, bm, final):
    n, d = h.shape
    row = lambda width: pl.BlockSpec((bm, width), lambda i: (i, 0))
    return pl.pallas_call(
        functools.partial(_ple_kernel, final=final),
        grid=(n // bm,),
        in_specs=[row(d), row(d), row(p.shape[1]), _const_spec(w["g_ple"].shape),
                  _const_spec(w["w_ple_gate"].shape), _const_spec(w["w_ple_proj"].shape),
                  _const_spec(g_final.shape)],
        out_specs=row(d),
        out_shape=jax.ShapeDtypeStruct((n, d), F32),
        compiler_params=_params("parallel"),
        name="ple",
    )(h, f, p, w["g_ple"], w["w_ple_gate"], w["w_ple_proj"], g_final)


def _prep_weights(i, g_mix, w_in, g_q, w_uq, g_kv, w_uk, w_uv, g_v, w_s, b_s, w_o, g_ffn,
                  w_ffn_gate, w_ffn_up, w_ffn_down, g_ple, w_ple_gate, w_ple_proj):
    q_rank = g_q.shape[1]
    kv_rank = g_kv.shape[1]
    heads, qk_dim = w_uq.shape[2], w_uq.shape[3]
    nope = w_uk.shape[3]
    rope_dim = qk_dim - nope
    groups = w_s.shape[1]
    cm_width = g_v.shape[1]
    wi = w_in[i]
    pe = wi[:, q_rank + kv_rank:q_rank + kv_rank + rope_dim]
    half = rope_dim // 2
    inv = ROPE_THETA ** (-jnp.arange(half, dtype=F32) / half)
    lane = jnp.arange(LANES)
    return {
        "g_mix": g_mix[i][None], "g_q": g_q[i][None], "g_kv": g_kv[i][None], "g_v": g_v[i][None],
        "g_ffn": g_ffn[i][None], "g_ple": g_ple[i][None],
        "w_a": wi[:, :q_rank + kv_rank].astype(BF16),
        "w_pe": jnp.concatenate([pe] * (LANES // rope_dim), axis=1).astype(BF16),
        "w_cm": wi[:, q_rank + kv_rank + rope_dim:].astype(BF16),
        "inv128": jnp.tile(inv, LANES // half)[None],
        "sign128": jnp.where((lane % rope_dim) < half, -1.0, 1.0).astype(F32)[None],
        "w_s": w_s[i], "b_sT": b_s[i].T,
        "w_s_diag": jnp.repeat(w_s[i][:, 0, 0], cm_width // groups)[None],
        "b_s_diag": jnp.repeat(b_s[i][:, 0], cm_width // groups)[None],
        "w_uq": jnp.concatenate([w_uq[i][:, :, :nope].reshape(q_rank, heads * nope),
                                 w_uq[i][:, :, nope:].reshape(q_rank, heads * rope_dim)],
                                axis=1).astype(BF16),
        "w_ukT": jnp.transpose(w_uk[i], (1, 2, 0)).astype(BF16),
        "w_uvT": jnp.transpose(w_uv[i], (1, 0, 2)).astype(BF16),
        "w_ple_proj": w_ple_proj[i].astype(BF16),
        "w_o": w_o[i], "w_gate": w_ffn_gate[i], "w_up": w_ffn_up[i], "w_down": w_ffn_down[i],
        "w_ple_gate": w_ple_gate[i],
    }


LATE_WEIGHTS = ("w_o", "w_gate", "w_up", "w_down", "w_ple_gate")


def kernel(x_prompt, x_sample, p_prompt, p_sample, cache_ckv, cache_kpe, page_table, g_mix, w_in,
           g_q, w_uq, g_kv, w_uk, w_uv, g_v, w_s, b_s, w_o, g_ffn, w_ffn_gate, w_ffn_up,
           w_ffn_down, g_ple, w_ple_gate, w_ple_proj, g_final):
    batch, seq, d_model = x_prompt.shape
    dec_batch, dec_seq, _ = x_sample.shape
    depth = g_mix.shape[0]
    q_rank, kv_rank = g_q.shape[1], g_kv.shape[1]
    heads, qk_dim = w_uq.shape[2], w_uq.shape[3]
    nope = w_uk.shape[3]
    rope_dim = qk_dim - nope
    cm_width = g_v.shape[1]
    groups, chunk = w_s.shape[1], w_s.shape[2]
    page = cache_ckv.shape[2]
    past_len = page_table.shape[1] * page
    scale = 1.0 / math.sqrt(nope + rope_dim)
    assert dec_seq == 1 and seq % chunk == 0
    in_dims = (q_rank, kv_rank, rope_dim, cm_width, chunk, groups)
    q_dims = (heads, nope, rope_dim, kv_rank)
    gfin = g_final[None]

    hp = x_prompt.reshape(batch * seq, d_model)
    hs = x_sample.reshape(dec_batch * dec_seq, d_model)
    outs = [[] for _ in range(6)]
    for i in range(depth):
        w = _prep_weights(i, g_mix, w_in, g_q, w_uq, g_kv, w_uk, w_uv, g_v, w_s, b_s, w_o, g_ffn,
                          w_ffn_gate, w_ffn_up, w_ffn_down, g_ple, w_ple_gate, w_ple_proj)

        cq, ckv, kpe, ckvb, kpeb, ocm, vn, cos, sin = _in_proj(
            hp, w, bm=512, seq_len=seq, pos0=0, dims=in_dims)
        qlat, qpe = _q_proj(cq, cos, sin, w, bm=512, dims=q_dims, scale=scale)
        oa, late = _attn_prompt(qlat, qpe, ckvb, kpeb, w["w_uvT"], [w[k] for k in LATE_WEIGHTS],
                                batch=batch, seq=seq, tq=128, tk=512)
        w.update(zip(LATE_WEIGHTS, late))
        h1, hf = _out_proj(hp, oa, ocm, w, bm=512)
        f = _ffn(hf, w, bm=1024, bh=512)
        hp = _ple(h1, f, p_prompt[i].reshape(batch * seq, -1), w, gfin, bm=512,
                  final=i == depth - 1)
        outs[0].append(ckv.reshape(batch, seq, kv_rank))
        outs[1].append(kpe.reshape(batch, seq, rope_dim))
        outs[2].append(vn.reshape(batch, seq, cm_width)[:, seq - chunk:])

        cq, ckv, kpe, ckvb, kpeb, ocm, vn, cos, sin = _in_proj(
            hs, w, bm=dec_batch, seq_len=dec_seq, pos0=past_len, dims=in_dims)
        qlat, qpe = _q_proj(cq, cos, sin, w, bm=dec_batch, dims=q_dims, scale=scale)
        o_lat = _attn_sample(jnp.transpose(qlat, (1, 0, 2)), jnp.transpose(qpe, (1, 0, 2)),
                             ckv[:, None, :], kpe[:, None, :], cache_ckv[i],
                             jnp.swapaxes(cache_kpe[i], 1, 2), page_table, cpp=16, nbuf=4)
        oa = _uv_proj(o_lat.reshape(dec_batch, heads * kv_rank), w["w_uvT"])
        h1, hf = _out_proj(hs, oa, ocm, w, bm=dec_batch)
        f = _ffn(hf, w, bm=dec_batch, bh=512)
        hs = _ple(h1, f, p_sample[i].reshape(dec_batch * dec_seq, -1), w, gfin, bm=dec_batch,
                  final=i == depth - 1)
        outs[3].append(ckv.reshape(dec_batch, dec_seq, kv_rank))
        outs[4].append(kpe.reshape(dec_batch, dec_seq, rope_dim))
        outs[5].append(vn.reshape(dec_batch, dec_seq, cm_width))

    y_prompt = hp.reshape(batch, seq, d_model)
    y_sample = hs.reshape(dec_batch, dec_seq, d_model)
    return (y_prompt, y_sample) + tuple(jnp.stack(o) for o in outs)
```

```python
import functools
import math

import jax
import jax.numpy as jnp
from jax import lax
from jax.experimental import pallas as pl
from jax.experimental.pallas import tpu as pltpu

F32 = jnp.float32
BF16 = jnp.bfloat16

EPS = 1e-6
NEG = -1e30
ROPE_THETA = 10000.0
LANES = 128
SUBLANES = 8
BF16_SUBLANES = 16
VMEM_LIMIT = 56 * 1024 * 1024


def _params(*sem):
    return pltpu.CompilerParams(dimension_semantics=sem, vmem_limit_bytes=VMEM_LIMIT)


def _rms(x, g):
    return x * lax.rsqrt(jnp.mean(x * x, axis=-1, keepdims=True) + EPS) * g


def _dot(a, b):
    return jnp.dot(a, b, preferred_element_type=F32)


def _dot_nt(a, b):
    return lax.dot_general(a, b, (((1,), (1,)), ((), ())), preferred_element_type=F32)


def _const_spec(shape):
    nd = len(shape)
    return pl.BlockSpec(shape, lambda *_: (0,) * nd)


def _in_proj_kernel(x_ref, gmix_ref, wa_ref, wpe_ref, wcm_ref, gq_ref, gkv_ref, gv_ref,
                    inv_ref, sign_ref, ws_ref, bs_ref,
                    cq_ref, ckv_ref, kpe_ref, ckvb_ref, kpeb_ref, ocm_ref, vn_ref,
                    cos_ref, sin_ref, *, bm, seq_len, pos0, q_rank, rope_dim, cm_width,
                    chunk, groups):
    i = pl.program_id(0)
    hn = _rms(x_ref[...], gmix_ref[...]).astype(BF16)

    za = _dot(hn, wa_ref[...])
    cq_ref[...] = _rms(za[:, :q_rank], gq_ref[...]).astype(BF16)
    ckv = _rms(za[:, q_rank:], gkv_ref[...])
    ckv_ref[...] = ckv
    ckvb_ref[...] = ckv.astype(BF16)

    row = i * bm + lax.broadcasted_iota(jnp.int32, (bm, LANES), 0)
    pos = pos0 + lax.rem(row, seq_len)
    ang = pos.astype(F32) * inv_ref[...]
    c = jnp.cos(ang)
    s = jnp.sin(ang) * sign_ref[...]
    cos_ref[...] = c
    sin_ref[...] = s
    zpe = _dot(hn, wpe_ref[...])
    kpe = (zpe * c + pltpu.roll(zpe, rope_dim // 2, axis=1) * s)[:, :rope_dim]
    kpe_ref[...] = kpe
    kpeb_ref[...] = kpe.astype(BF16)

    gz = jax.nn.gelu(_dot(hn, wcm_ref[...]))
    u = gz[:, :cm_width]
    vn = _rms(gz[:, cm_width:], gv_ref[...])
    vn_ref[...] = vn
    if seq_len % chunk == 0:
        vb = vn.astype(BF16)
        r = lax.broadcasted_iota(jnp.int32, (chunk, chunk), 0)
        cc = lax.broadcasted_iota(jnp.int32, (chunk, chunk), 1)
        head = cm_width // groups
        for g in range(groups):
            wsg = jnp.where(r >= cc, ws_ref[g], 0.0).astype(BF16)
            bcol = bs_ref[:, g:g + 1]
            for k in range(bm // chunk):
                rows = slice(k * chunk, (k + 1) * chunk)
                cols = slice(g * head, (g + 1) * head)
                mixed = _dot(wsg, vb[rows, cols]) + bcol
                ocm_ref[rows, cols] = (u[rows, cols] * mixed).astype(BF16)
    else:
        ocm_ref[...] = (u * (vn * ws_ref[...] + bs_ref[...])).astype(BF16)


def _in_proj(x, w, *, bm, seq_len, pos0, dims):
    n, d = x.shape
    q_rank, kv_rank, rope_dim, cm_width, chunk, groups = dims
    chunked = seq_len % chunk == 0
    assert chunked or seq_len == 1
    if chunked:
        ws, bs = w["w_s"], w["b_sT"]
    else:
        ws, bs = w["w_s_diag"], w["b_s_diag"]
    kern = functools.partial(_in_proj_kernel, bm=bm, seq_len=seq_len, pos0=pos0, q_rank=q_rank,
                             rope_dim=rope_dim, cm_width=cm_width, chunk=chunk, groups=groups)
    row = lambda width: pl.BlockSpec((bm, width), lambda i: (i, 0))
    consts = [w["g_mix"], w["w_a"], w["w_pe"], w["w_cm"], w["g_q"], w["g_kv"], w["g_v"],
              w["inv128"], w["sign128"], ws, bs]
    out_shape = (
        jax.ShapeDtypeStruct((n, q_rank), BF16),
        jax.ShapeDtypeStruct((n, kv_rank), F32),
        jax.ShapeDtypeStruct((n, rope_dim), F32),
        jax.ShapeDtypeStruct((n, kv_rank), BF16),
        jax.ShapeDtypeStruct((n, rope_dim), BF16),
        jax.ShapeDtypeStruct((n, cm_width), BF16),
        jax.ShapeDtypeStruct((n, cm_width), F32),
        jax.ShapeDtypeStruct((n, LANES), F32),
        jax.ShapeDtypeStruct((n, LANES), F32),
    )
    return pl.pallas_call(
        kern,
        grid=(n // bm,),
        in_specs=[row(d)] + [_const_spec(a.shape) for a in consts],
        out_specs=tuple(row(s.shape[1]) for s in out_shape),
        out_shape=out_shape,
        compiler_params=_params("parallel"),
        name="in_proj",
    )(x, *consts)


def _q_proj_kernel(cq_ref, wuq_ref, wuk_ref, cos_ref, sin_ref, qlat_ref, qpe_ref, *,
                   heads, nope, rope_dim, scale):
    q = _dot(cq_ref[...], wuq_ref[...])
    qn = q[:, :heads * nope].astype(BF16)
    for h in range(heads):
        qlat_ref[h] = (_dot(qn[:, h * nope:(h + 1) * nope], wuk_ref[h]) * scale).astype(BF16)
    qp = q[:, heads * nope:]
    width = heads * rope_dim
    reps = width // LANES
    c = jnp.concatenate([cos_ref[...]] * reps, axis=1)
    s = jnp.concatenate([sin_ref[...]] * reps, axis=1)
    half = rope_dim // 2
    lane = lax.broadcasted_iota(jnp.int32, qp.shape, 1)
    first_half = lax.rem(lane, rope_dim) < half
    partner = jnp.where(first_half, pltpu.roll(qp, width - half, axis=1),
                        pltpu.roll(qp, half, axis=1))
    qr = (qp * c + partner * s) * scale
    for h in range(heads):
        qpe_ref[h] = qr[:, h * rope_dim:(h + 1) * rope_dim].astype(BF16)


def _q_proj(cq, cos, sin, w, *, bm, dims, scale):
    n, q_rank = cq.shape
    heads, nope, rope_dim, kv_rank = dims
    kern = functools.partial(_q_proj_kernel, heads=heads, nope=nope, rope_dim=rope_dim,
                             scale=scale)
    row = lambda width: pl.BlockSpec((bm, width), lambda i: (i, 0))
    return pl.pallas_call(
        kern,
        grid=(n // bm,),
        in_specs=[row(q_rank), _const_spec(w["w_uq"].shape), _const_spec(w["w_ukT"].shape),
                  row(LANES), row(LANES)],
        out_specs=(pl.BlockSpec((heads, bm, kv_rank), lambda i: (0, i, 0)),
                   pl.BlockSpec((heads, bm, rope_dim), lambda i: (0, i, 0))),
        out_shape=(jax.ShapeDtypeStruct((heads, n, kv_rank), BF16),
                   jax.ShapeDtypeStruct((heads, n, rope_dim), BF16)),
        compiler_params=_params("parallel"),
        name="q_proj",
    )(cq, w["w_uq"], w["w_ukT"], cos, sin)


def _prompt_unit(qi, qlat_ref, qpe_ref, kc_ref, kp_ref, wuv_ref, o_ref, m_ref, l_ref, acc_ref,
                 *, tq, tk):
    heads, _, kv_rank = qlat_ref.shape
    rope_dim = qpe_ref.shape[-1]
    vdim = wuv_ref.shape[-1]
    rows = heads * tq
    q = qlat_ref[...].reshape(rows, kv_rank)
    qp = qpe_ref[...].reshape(rows, rope_dim)
    m_ref[...] = jnp.full(m_ref.shape, NEG, F32)
    l_ref[...] = jnp.zeros(l_ref.shape, F32)
    acc_ref[...] = jnp.zeros(acc_ref.shape, F32)
    n_full = (qi * tq) // tk

    def step(kb, masked):
        k0 = pl.multiple_of(kb * tk, tk)
        kc = kc_ref[pl.ds(k0, tk), :]
        kp = kp_ref[pl.ds(k0, tk), :]
        s = _dot_nt(q, kc) + _dot_nt(qp, kp)
        if masked:
            qpos = qi * tq + lax.rem(lax.broadcasted_iota(jnp.int32, (rows, tk), 0), tq)
            col = lax.broadcasted_iota(jnp.int32, (rows, tk), 1)
            s = jnp.where(k0 + col <= qpos, s, NEG)
        m_old = m_ref[...]
        m_new = jnp.maximum(m_old, jnp.max(s, axis=1, keepdims=True))
        corr = jnp.exp(m_old - m_new)
        p = jnp.exp(s - jnp.concatenate([m_new] * (tk // LANES), axis=1))
        psum = p[:, :LANES]
        for t in range(1, tk // LANES):
            psum = psum + p[:, t * LANES:(t + 1) * LANES]
        l_ref[...] = l_ref[...] * corr + psum
        acc_ref[...] = (acc_ref[...] * jnp.concatenate([corr] * (kv_rank // LANES), axis=1)
                        + _dot(p.astype(BF16), kc))
        m_ref[...] = m_new

    def body(kb, carry):
        step(kb, False)
        return carry

    lax.fori_loop(0, n_full, body, 0)
    step(n_full, True)
    o = (acc_ref[...] / jnp.sum(l_ref[...], axis=1, keepdims=True)).astype(BF16)
    for h in range(heads):
        o_ref[:, h * vdim:(h + 1) * vdim] = _dot(o[h * tq:(h + 1) * tq], wuv_ref[h]).astype(BF16)


def _sample_seq(b, nb, pt_ref, qlat_ref, qpe_ref, cnew_ref, knew_ref, ckv_hbm, kpe_hbm, o_ref,
                kbuf, pbuf, sem_k, sem_p, *, cpp, nbuf, n_chunks):
    total = nb * n_chunks
    page, kv_rank = kbuf.shape[2], kbuf.shape[3]
    ahead = nbuf - 2

    def copies(g, slot):
        out = []
        for j in range(cpp):
            pid = pt_ref[g * cpp + j]
            out.append(pltpu.make_async_copy(ckv_hbm.at[pid], kbuf.at[slot, j], sem_k.at[slot, j]))
            out.append(pltpu.make_async_copy(kpe_hbm.at[pid], pbuf.at[slot, j], sem_p.at[slot, j]))
        return out

    @pl.when(b == 0)
    def _():
        for g in range(ahead):
            for cp in copies(g, g % nbuf):
                cp.start()

    q = qlat_ref[...]
    qp = qpe_ref[...]
    heads = q.shape[0]
    m = jnp.full((heads, 1), NEG, F32)
    l = jnp.zeros((heads, 1), F32)
    acc = jnp.zeros((heads, kv_rank), F32)

    def fetch(c):
        g = b * n_chunks + c
        slot = c % nbuf
        for cp in copies(g, slot):
            cp.wait()
        for cp in copies(jnp.minimum(g + ahead, total - 1), (c + ahead) % nbuf):
            cp.start()
        kc = kbuf[slot].reshape(cpp * page, kv_rank).astype(BF16)
        kp = jnp.concatenate([pbuf[slot, j] for j in range(cpp)], axis=1).astype(BF16)
        return kc, _dot_nt(q, kc) + _dot(qp, kp)

    kc, s = fetch(0)
    for c in range(n_chunks):
        if c + 1 < n_chunks:
            kc_next, s_next = fetch(c + 1)
        m_new = jnp.maximum(m, jnp.max(s, axis=1, keepdims=True))
        corr = jnp.exp(m - m_new)
        p = jnp.exp(s - m_new)
        l = l * corr + jnp.sum(p, axis=1, keepdims=True)
        acc = acc * corr + _dot(p.astype(BF16), kc)
        m = m_new
        if c + 1 < n_chunks:
            kc, s = kc_next, s_next

    @pl.when(b == nb - 1)
    def _():
        for k in range(ahead):
            for cp in copies(total - 1, (n_chunks + k) % nbuf):
                cp.wait()

    cn = cnew_ref[...]
    s_new = (jnp.sum(q.astype(F32) * cn, axis=1, keepdims=True)
             + jnp.sum(qp.astype(F32) * knew_ref[...], axis=1, keepdims=True))
    m_tot = jnp.maximum(m, s_new)
    c_old = jnp.exp(m - m_tot)
    p_new = jnp.exp(s_new - m_tot)
    o_ref[...] = (acc * c_old + p_new * cn) / (l * c_old + p_new)


def _attn_kernel(pt_ref, sq_ref, sqp_ref, cnew_ref, knew_ref, pq_ref, pqp_ref, kc_ref, kp_ref,
                 wuv_ref, *rest, n_cast, nq, split, tq, tk, cpp, nbuf, n_chunks):
    cast_in = rest[:n_cast]
    ckv_hbm, kpe_hbm, so_ref, po_ref = rest[n_cast:n_cast + 4]
    cast_out = rest[n_cast + 4:2 * n_cast + 4]
    kbuf, pbuf, sem_k, sem_p, m_ref, l_ref, acc_ref = rest[2 * n_cast + 4:]
    b = pl.program_id(0)
    for src, dst in zip(cast_in, cast_out):
        dst[...] = src[...].astype(BF16)
    _prompt_unit(lax.rem(b // split, nq), pq_ref, pqp_ref, kc_ref, kp_ref, wuv_ref, po_ref,
                 m_ref, l_ref, acc_ref, tq=tq, tk=tk)
    _sample_seq(b, pl.num_programs(0), pt_ref, sq_ref, sqp_ref, cnew_ref, knew_ref, ckv_hbm,
                kpe_hbm, so_ref, kbuf, pbuf, sem_k, sem_p, cpp=cpp, nbuf=nbuf, n_chunks=n_chunks)


def _slab_spec(arr, steps):
    r, c = arr.shape
    n_slabs = steps
    while r % n_slabs or (r // n_slabs) % BF16_SUBLANES:
        n_slabs //= 2
    assert n_slabs >= 1
    return pl.BlockSpec((r // n_slabs, c), lambda b, pt: (jnp.minimum(b, n_slabs - 1), 0))


def _attention(sq, sqp, cnew, knew, cache_ckv, cache_kpe_t, page_table, pq, pqp, kc, kp, wuv,
               to_cast, *, batch, seq, tq, tk, cpp, nbuf):
    bd, heads, kv_rank = sq.shape
    rope_dim = sqp.shape[-1]
    n = pq.shape[1]
    vdim = wuv.shape[-1]
    nq = seq // tq
    split = bd // (batch * nq)
    assert split * batch * nq == bd and heads % split == 0
    hpg = heads // split
    assert tk % tq == 0 and seq % tk == 0
    n_pages = page_table.shape[1]
    page = cache_ckv.shape[1]
    n_chunks = n_pages // cpp
    assert n_pages % cpp == 0 and n_chunks % nbuf == 0 and nbuf >= 3
    pt = page_table.reshape(-1)
    kern = functools.partial(_attn_kernel, n_cast=len(to_cast), nq=nq, split=split, tq=tq, tk=tk,
                             cpp=cpp, nbuf=nbuf, n_chunks=n_chunks)
    per_b = lambda r, width: pl.BlockSpec((None, r, width), lambda b, pt_ref: (b, 0, 0))
    slabs = [_slab_spec(a, bd) for a in to_cast]
    any_spec = pl.BlockSpec(memory_space=pl.ANY)
    grid_spec = pltpu.PrefetchScalarGridSpec(
        num_scalar_prefetch=1,
        grid=(bd,),
        in_specs=[per_b(heads, kv_rank), per_b(heads, rope_dim), per_b(1, kv_rank),
                  per_b(1, rope_dim),
                  pl.BlockSpec((hpg, tq, kv_rank), lambda b, pt_ref: (b % split, b // split, 0)),
                  pl.BlockSpec((hpg, tq, rope_dim), lambda b, pt_ref: (b % split, b // split, 0)),
                  pl.BlockSpec((seq, kv_rank), lambda b, pt_ref: (b // (split * nq), 0)),
                  pl.BlockSpec((seq, rope_dim), lambda b, pt_ref: (b // (split * nq), 0)),
                  pl.BlockSpec((hpg,) + wuv.shape[1:], lambda b, pt_ref: (b % split, 0, 0))]
                 + slabs + [any_spec, any_spec],
        out_specs=[per_b(heads, kv_rank),
                   pl.BlockSpec((tq, hpg * vdim), lambda b, pt_ref: (b // split, b % split))]
                  + slabs,
        scratch_shapes=[pltpu.VMEM((nbuf, cpp, page, kv_rank), F32),
                        pltpu.VMEM((nbuf, cpp, rope_dim, page), F32),
                        pltpu.SemaphoreType.DMA((nbuf, cpp)), pltpu.SemaphoreType.DMA((nbuf, cpp)),
                        pltpu.VMEM((hpg * tq, LANES), F32), pltpu.VMEM((hpg * tq, LANES), F32),
                        pltpu.VMEM((hpg * tq, kv_rank), F32)],
    )
    outs = pl.pallas_call(
        kern,
        grid_spec=grid_spec,
        out_shape=[jax.ShapeDtypeStruct((bd, heads, kv_rank), F32),
                   jax.ShapeDtypeStruct((n, heads * vdim), BF16)]
                  + [jax.ShapeDtypeStruct(a.shape, BF16) for a in to_cast],
        compiler_params=_params("arbitrary"),
        name="attention",
    )(pt, sq, sqp, cnew, knew, pq, pqp, kc, kp, wuv, *to_cast, cache_ckv, cache_kpe_t)
    return outs[0], outs[1], outs[2:]


def _uv_kernel(o_ref, wuv_ref, out_ref, *, heads):
    kv_rank = wuv_ref.shape[1]
    vdim = wuv_ref.shape[2]
    for h in range(heads):
        oh = o_ref[:, h * kv_rank:(h + 1) * kv_rank].astype(BF16)
        out_ref[:, h * vdim:(h + 1) * vdim] = _dot(oh, wuv_ref[h]).astype(BF16)


def _uv_proj(o_lat, wuv):
    n = o_lat.shape[0]
    heads, kv_rank, vdim = wuv.shape
    return pl.pallas_call(
        functools.partial(_uv_kernel, heads=heads),
        grid=(1,),
        in_specs=[_const_spec(o_lat.shape), _const_spec(wuv.shape)],
        out_specs=_const_spec((n, heads * vdim)),
        out_shape=jax.ShapeDtypeStruct((n, heads * vdim), BF16),
        compiler_params=_params("arbitrary"),
        name="uv_proj",
    )(o_lat, wuv)


def _out_proj_kernel(x_ref, oa_ref, ocm_ref, wo_ref, gffn_ref, h_ref, hf_ref):
    wa = oa_ref.shape[1]
    h1 = x_ref[...] + _dot(oa_ref[...], wo_ref[:wa, :]) + _dot(ocm_ref[...], wo_ref[wa:, :])
    h_ref[...] = h1
    hf_ref[...] = _rms(h1, gffn_ref[...]).astype(BF16)


def _out_proj(x, oa, ocm, w, *, bm):
    n, d = x.shape
    row = lambda width: pl.BlockSpec((bm, width), lambda i: (i, 0))
    return pl.pallas_call(
        _out_proj_kernel,
        grid=(n // bm,),
        in_specs=[row(d), row(oa.shape[1]), row(ocm.shape[1]), _const_spec(w["w_o"].shape),
                  _const_spec(w["g_ffn"].shape)],
        out_specs=(row(d), row(d)),
        out_shape=(jax.ShapeDtypeStruct((n, d), F32), jax.ShapeDtypeStruct((n, d), BF16)),
        compiler_params=_params("parallel"),
        name="out_proj",
    )(x, oa, ocm, w["w_o"], w["g_ffn"])


def _ffn_kernel(hf_ref, wg_ref, wu_ref, wd_ref, out_ref):
    j = pl.program_id(1)

    @pl.when(j == 0)
    def _():
        out_ref[...] = jnp.zeros(out_ref.shape, F32)

    hf = hf_ref[...]
    t = (jax.nn.silu(_dot(hf, wg_ref[...])) * _dot(hf, wu_ref[...])).astype(BF16)
    out_ref[...] += _dot(t, wd_ref[...])


def _ffn(hf, w, *, bm, bh):
    n, d = hf.shape
    hidden = w["w_gate"].shape[1]
    assert hidden % bh == 0 and n % bm == 0
    return pl.pallas_call(
        _ffn_kernel,
        grid=(n // bm, hidden // bh),
        in_specs=[pl.BlockSpec((bm, d), lambda i, j: (i, 0)),
                  pl.BlockSpec((d, bh), lambda i, j: (0, j)),
                  pl.BlockSpec((d, bh), lambda i, j: (0, j)),
                  pl.BlockSpec((bh, d), lambda i, j: (j, 0))],
        out_specs=pl.BlockSpec((bm, d), lambda i, j: (i, 0)),
        out_shape=jax.ShapeDtypeStruct((n, d), F32),
        compiler_params=_params("parallel", "arbitrary"),
        name="ffn",
    )(hf, w["w_gate"], w["w_up"], w["w_down"])


def _ple_kernel(h_ref, f_ref, p_ref, gple_ref, wg_ref, wp_ref, gfin_ref, y_ref, *, final):
    h = h_ref[...] + f_ref[...]
    gate = jax.nn.sigmoid(_dot(_rms(h, gple_ref[...]).astype(BF16), wg_ref[...]))
    h3 = h + gate * _dot(p_ref[...].astype(BF16), wp_ref[...])
    y_ref[...] = _rms(h3, gfin_ref[...]) if final else h3


def _ple(h, f, p, w, g_final, *, bm, final):
    n, d = h.shape
    row = lambda width: pl.BlockSpec((bm, width), lambda i: (i, 0))
    return pl.pallas_call(
        functools.partial(_ple_kernel, final=final),
        grid=(n // bm,),
        in_specs=[row(d), row(d), row(p.shape[1]), _const_spec(w["g_ple"].shape),
                  _const_spec(w["w_ple_gate"].shape), _const_spec(w["w_ple_proj"].shape),
                  _const_spec(g_final.shape)],
        out_specs=row(d),
        out_shape=jax.ShapeDtypeStruct((n, d), F32),
        compiler_params=_params("parallel"),
        name="ple",
    )(h, f, p, w["g_ple"], w["w_ple_gate"], w["w_ple_proj"], g_final)


def _prep_weights(i, g_mix, w_in, g_q, w_uq, g_kv, w_uk, w_uv, g_v, w_s, b_s, w_o, g_ffn,
                  w_ffn_gate, w_ffn_up, w_ffn_down, g_ple, w_ple_gate, w_ple_proj):
    q_rank = g_q.shape[1]
    kv_rank = g_kv.shape[1]
    heads, qk_dim = w_uq.shape[2], w_uq.shape[3]
    nope = w_uk.shape[3]
    rope_dim = qk_dim - nope
    groups = w_s.shape[1]
    cm_width = g_v.shape[1]
    wi = w_in[i]
    pe = wi[:, q_rank + kv_rank:q_rank + kv_rank + rope_dim]
    half = rope_dim // 2
    inv = ROPE_THETA ** (-jnp.arange(half, dtype=F32) / half)
    lane = jnp.arange(LANES)
    return {
        "g_mix": g_mix[i][None], "g_q": g_q[i][None], "g_kv": g_kv[i][None], "g_v": g_v[i][None],
        "g_ffn": g_ffn[i][None], "g_ple": g_ple[i][None],
        "w_a": wi[:, :q_rank + kv_rank].astype(BF16),
        "w_pe": jnp.concatenate([pe] * (LANES // rope_dim), axis=1).astype(BF16),
        "w_cm": wi[:, q_rank + kv_rank + rope_dim:].astype(BF16),
        "inv128": jnp.tile(inv, LANES // half)[None],
        "sign128": jnp.where((lane % rope_dim) < half, -1.0, 1.0).astype(F32)[None],
        "w_s": w_s[i], "b_sT": b_s[i].T,
        "w_s_diag": jnp.repeat(w_s[i][:, 0, 0], cm_width // groups)[None],
        "b_s_diag": jnp.repeat(b_s[i][:, 0], cm_width // groups)[None],
        "w_uq": jnp.concatenate([w_uq[i][:, :, :nope].reshape(q_rank, heads * nope),
                                 w_uq[i][:, :, nope:].reshape(q_rank, heads * rope_dim)],
                                axis=1).astype(BF16),
        "w_ukT": jnp.transpose(w_uk[i], (1, 2, 0)).astype(BF16),
        "w_uvT": jnp.transpose(w_uv[i], (1, 0, 2)).astype(BF16),
        "w_ple_proj": w_ple_proj[i].astype(BF16),
        "w_o": w_o[i], "w_gate": w_ffn_gate[i], "w_up": w_ffn_up[i], "w_down": w_ffn_down[i],
        "w_ple_gate": w_ple_gate[i],
    }


LATE_WEIGHTS = ("w_o", "w_gate", "w_up", "w_down", "w_ple_gate")


def kernel(x_prompt, x_sample, p_prompt, p_sample, cache_ckv, cache_kpe, page_table, g_mix, w_in,
           g_q, w_uq, g_kv, w_uk, w_uv, g_v, w_s, b_s, w_o, g_ffn, w_ffn_gate, w_ffn_up,
           w_ffn_down, g_ple, w_ple_gate, w_ple_proj, g_final):
    batch, seq, d_model = x_prompt.shape
    dec_batch, dec_seq, _ = x_sample.shape
    depth = g_mix.shape[0]
    q_rank, kv_rank = g_q.shape[1], g_kv.shape[1]
    heads, qk_dim = w_uq.shape[2], w_uq.shape[3]
    nope = w_uk.shape[3]
    rope_dim = qk_dim - nope
    cm_width = g_v.shape[1]
    groups, chunk = w_s.shape[1], w_s.shape[2]
    page = cache_ckv.shape[2]
    past_len = page_table.shape[1] * page
    scale = 1.0 / math.sqrt(nope + rope_dim)
    assert dec_seq == 1 and seq % chunk == 0
    in_dims = (q_rank, kv_rank, rope_dim, cm_width, chunk, groups)
    q_dims = (heads, nope, rope_dim, kv_rank)
    gfin = g_final[None]

    hp = x_prompt.reshape(batch * seq, d_model)
    hs = x_sample.reshape(dec_batch * dec_seq, d_model)
    outs = [[] for _ in range(6)]
    for i in range(depth):
        w = _prep_weights(i, g_mix, w_in, g_q, w_uq, g_kv, w_uk, w_uv, g_v, w_s, b_s, w_o, g_ffn,
                          w_ffn_gate, w_ffn_up, w_ffn_down, g_ple, w_ple_gate, w_ple_proj)

        cq, ckv_p, kpe_p, ckvb, kpeb, ocm_p, vn_p, cos, sin = _in_proj(
            hp, w, bm=512, seq_len=seq, pos0=0, dims=in_dims)
        qlat_p, qpe_p = _q_proj(cq, cos, sin, w, bm=512, dims=q_dims, scale=scale)
        cq, ckv_s, kpe_s, _, _, ocm_s, vn_s, cos, sin = _in_proj(
            hs, w, bm=dec_batch, seq_len=dec_seq, pos0=past_len, dims=in_dims)
        qlat_s, qpe_s = _q_proj(cq, cos, sin, w, bm=dec_batch, dims=q_dims, scale=scale)
        o_lat, oa_p, late = _attention(
            jnp.transpose(qlat_s, (1, 0, 2)), jnp.transpose(qpe_s, (1, 0, 2)),
            ckv_s[:, None, :], kpe_s[:, None, :], cache_ckv[i], jnp.swapaxes(cache_kpe[i], 1, 2),
            page_table, qlat_p, qpe_p, ckvb, kpeb, w["w_uvT"], [w[k] for k in LATE_WEIGHTS],
            batch=batch, seq=seq, tq=128, tk=512, cpp=8, nbuf=8)
        w.update(zip(LATE_WEIGHTS, late))

        h1, hf = _out_proj(hp, oa_p, ocm_p, w, bm=512)
        f = _ffn(hf, w, bm=1024, bh=512)
        hp = _ple(h1, f, p_prompt[i].reshape(batch * seq, -1), w, gfin, bm=512,
                  final=i == depth - 1)
        outs[0].append(ckv_p.reshape(batch, seq, kv_rank))
        outs[1].append(kpe_p.reshape(batch, seq, rope_dim))
        outs[2].append(vn_p.reshape(batch, seq, cm_width)[:, seq - chunk:])

        oa_s = _uv_proj(o_lat.reshape(dec_batch, heads * kv_rank), w["w_uvT"])
        h1, hf = _out_proj(hs, oa_s, ocm_s, w, bm=dec_batch)
        f = _ffn(hf, w, bm=dec_batch, bh=512)
        hs = _ple(h1, f, p_sample[i].reshape(dec_batch * dec_seq, -1), w, gfin, bm=dec_batch,
                  final=i == depth - 1)
        outs[3].append(ckv_s.reshape(dec_batch, dec_seq, kv_rank))
        outs[4].append(kpe_s.reshape(dec_batch, dec_seq, rope_dim))
        outs[5].append(vn_s.reshape(dec_batch, dec_seq, cm_width))

    y_prompt = hp.reshape(batch, seq, d_model)
    y_sample = hs.reshape(dec_batch, dec_seq, d_model)
    return (y_prompt, y_sample) + tuple(jnp.stack(o) for o in outs)
```

```python
import functools
import math

import jax
import jax.numpy as jnp
from jax import lax
from jax.experimental import pallas as pl
from jax.experimental.pallas import tpu as pltpu

F32 = jnp.float32
BF16 = jnp.bfloat16

EPS = 1e-6
NEG = -1e30
ROPE_THETA = 10000.0
LANES = 128
SUBLANES = 8
BF16_SUBLANES = 16
VMEM_LIMIT = 56 * 1024 * 1024


def _params(*sem):
    return pltpu.CompilerParams(dimension_semantics=sem, vmem_limit_bytes=VMEM_LIMIT)


def _rms(x, g):
    return x * lax.rsqrt(jnp.mean(x * x, axis=-1, keepdims=True) + EPS) * g


def _dot(a, b):
    return jnp.dot(a, b, preferred_element_type=F32)


def _dot_nt(a, b):
    return lax.dot_general(a, b, (((1,), (1,)), ((), ())), preferred_element_type=F32)


def _const_spec(shape):
    nd = len(shape)
    return pl.BlockSpec(shape, lambda *_: (0,) * nd)


def _in_proj_kernel(x_ref, gmix_ref, wa_ref, wpe_ref, wcm_ref, gq_ref, gkv_ref, gv_ref,
                    inv_ref, sign_ref, ws_ref, bs_ref,
                    cq_ref, ckv_ref, kpe_ref, ckvb_ref, kpeb_ref, ocm_ref, vn_ref,
                    cos_ref, sin_ref, *, bm, seq_len, pos0, q_rank, rope_dim, cm_width,
                    chunk, groups):
    i = pl.program_id(0)
    hn = _rms(x_ref[...], gmix_ref[...]).astype(BF16)

    za = _dot(hn, wa_ref[...])
    cq_ref[...] = _rms(za[:, :q_rank], gq_ref[...]).astype(BF16)
    ckv = _rms(za[:, q_rank:], gkv_ref[...])
    ckv_ref[...] = ckv
    ckvb_ref[...] = ckv.astype(BF16)

    row = i * bm + lax.broadcasted_iota(jnp.int32, (bm, LANES), 0)
    pos = pos0 + lax.rem(row, seq_len)
    ang = pos.astype(F32) * inv_ref[...]
    c = jnp.cos(ang)
    s = jnp.sin(ang) * sign_ref[...]
    cos_ref[...] = c
    sin_ref[...] = s
    zpe = _dot(hn, wpe_ref[...])
    kpe = (zpe * c + pltpu.roll(zpe, rope_dim // 2, axis=1) * s)[:, :rope_dim]
    kpe_ref[...] = kpe
    kpeb_ref[...] = kpe.astype(BF16)

    gz = jax.nn.gelu(_dot(hn, wcm_ref[...]))
    u = gz[:, :cm_width]
    vn = _rms(gz[:, cm_width:], gv_ref[...])
    vn_ref[...] = vn
    if seq_len % chunk == 0:
        vb = vn.astype(BF16)
        r = lax.broadcasted_iota(jnp.int32, (chunk, chunk), 0)
        cc = lax.broadcasted_iota(jnp.int32, (chunk, chunk), 1)
        head = cm_width // groups
        for g in range(groups):
            wsg = jnp.where(r >= cc, ws_ref[g], 0.0).astype(BF16)
            bcol = bs_ref[:, g:g + 1]
            for k in range(bm // chunk):
                rows = slice(k * chunk, (k + 1) * chunk)
                cols = slice(g * head, (g + 1) * head)
                mixed = _dot(wsg, vb[rows, cols]) + bcol
                ocm_ref[rows, cols] = (u[rows, cols] * mixed).astype(BF16)
    else:
        ocm_ref[...] = (u * (vn * ws_ref[...] + bs_ref[...])).astype(BF16)


def _in_proj(x, w, *, bm, seq_len, pos0, dims):
    n, d = x.shape
    q_rank, kv_rank, rope_dim, cm_width, chunk, groups = dims
    chunked = seq_len % chunk == 0
    assert chunked or seq_len == 1
    if chunked:
        ws, bs = w["w_s"], w["b_sT"]
    else:
        ws, bs = w["w_s_diag"], w["b_s_diag"]
    kern = functools.partial(_in_proj_kernel, bm=bm, seq_len=seq_len, pos0=pos0, q_rank=q_rank,
                             rope_dim=rope_dim, cm_width=cm_width, chunk=chunk, groups=groups)
    row = lambda width: pl.BlockSpec((bm, width), lambda i: (i, 0))
    consts = [w["g_mix"], w["w_a"], w["w_pe"], w["w_cm"], w["g_q"], w["g_kv"], w["g_v"],
              w["inv128"], w["sign128"], ws, bs]
    out_shape = (
        jax.ShapeDtypeStruct((n, q_rank), BF16),
        jax.ShapeDtypeStruct((n, kv_rank), F32),
        jax.ShapeDtypeStruct((n, rope_dim), F32),
        jax.ShapeDtypeStruct((n, kv_rank), BF16),
        jax.ShapeDtypeStruct((n, rope_dim), BF16),
        jax.ShapeDtypeStruct((n, cm_width), BF16),
        jax.ShapeDtypeStruct((n, cm_width), F32),
        jax.ShapeDtypeStruct((n, LANES), F32),
        jax.ShapeDtypeStruct((n, LANES), F32),
    )
    return pl.pallas_call(
        kern,
        grid=(n // bm,),
        in_specs=[row(d)] + [_const_spec(a.shape) for a in consts],
        out_specs=tuple(row(s.shape[1]) for s in out_shape),
        out_shape=out_shape,
        compiler_params=_params("parallel"),
        name="in_proj",
    )(x, *consts)


def _q_proj_kernel(cq_ref, wuq_ref, wuk_ref, cos_ref, sin_ref, qlat_ref, qpe_ref, *,
                   heads, nope, rope_dim, scale):
    q = _dot(cq_ref[...], wuq_ref[...])
    qn = q[:, :heads * nope].astype(BF16)
    for h in range(heads):
        qlat_ref[h] = (_dot(qn[:, h * nope:(h + 1) * nope], wuk_ref[h]) * scale).astype(BF16)
    qp = q[:, heads * nope:]
    width = heads * rope_dim
    reps = width // LANES
    c = jnp.concatenate([cos_ref[...]] * reps, axis=1)
    s = jnp.concatenate([sin_ref[...]] * reps, axis=1)
    half = rope_dim // 2
    lane = lax.broadcasted_iota(jnp.int32, qp.shape, 1)
    first_half = lax.rem(lane, rope_dim) < half
    partner = jnp.where(first_half, pltpu.roll(qp, width - half, axis=1),
                        pltpu.roll(qp, half, axis=1))
    qr = (qp * c + partner * s) * scale
    for h in range(heads):
        qpe_ref[h] = qr[:, h * rope_dim:(h + 1) * rope_dim].astype(BF16)


def _q_proj(cq, cos, sin, w, *, bm, dims, scale):
    n, q_rank = cq.shape
    heads, nope, rope_dim, kv_rank = dims
    kern = functools.partial(_q_proj_kernel, heads=heads, nope=nope, rope_dim=rope_dim,
                             scale=scale)
    row = lambda width: pl.BlockSpec((bm, width), lambda i: (i, 0))
    return pl.pallas_call(
        kern,
        grid=(n // bm,),
        in_specs=[row(q_rank), _const_spec(w["w_uq"].shape), _const_spec(w["w_ukT"].shape),
                  row(LANES), row(LANES)],
        out_specs=(pl.BlockSpec((heads, bm, kv_rank), lambda i: (0, i, 0)),
                   pl.BlockSpec((heads, bm, rope_dim), lambda i: (0, i, 0))),
        out_shape=(jax.ShapeDtypeStruct((heads, n, kv_rank), BF16),
                   jax.ShapeDtypeStruct((heads, n, rope_dim), BF16)),
        compiler_params=_params("parallel"),
        name="q_proj",
    )(cq, w["w_uq"], w["w_ukT"], cos, sin)


def _prompt_unit(qi, qlat_ref, qpe_ref, kc_ref, kp_ref, wuv_ref, o_ref, m_ref, l_ref, acc_ref,
                 *, tq, tk):
    heads, _, kv_rank = qlat_ref.shape
    rope_dim = qpe_ref.shape[-1]
    vdim = wuv_ref.shape[-1]
    rows = heads * tq
    q = qlat_ref[...].reshape(rows, kv_rank)
    qp = qpe_ref[...].reshape(rows, rope_dim)
    m_ref[...] = jnp.full(m_ref.shape, NEG, F32)
    l_ref[...] = jnp.zeros(l_ref.shape, F32)
    acc_ref[...] = jnp.zeros(acc_ref.shape, F32)
    n_full = (qi * tq) // tk

    def step(kb, masked):
        k0 = pl.multiple_of(kb * tk, tk)
        kc = kc_ref[pl.ds(k0, tk), :]
        kp = kp_ref[pl.ds(k0, tk), :]
        s = _dot_nt(q, kc) + _dot_nt(qp, kp)
        if masked:
            qpos = qi * tq + lax.rem(lax.broadcasted_iota(jnp.int32, (rows, tk), 0), tq)
            col = lax.broadcasted_iota(jnp.int32, (rows, tk), 1)
            s = jnp.where(k0 + col <= qpos, s, NEG)
        m_old = m_ref[...]
        m_new = jnp.maximum(m_old, jnp.max(s, axis=1, keepdims=True))
        corr = jnp.exp(m_old - m_new)
        p = jnp.exp(s - jnp.concatenate([m_new] * (tk // LANES), axis=1))
        psum = p[:, :LANES]
        for t in range(1, tk // LANES):
            psum = psum + p[:, t * LANES:(t + 1) * LANES]
        l_ref[...] = l_ref[...] * corr + psum
        acc_ref[...] = (acc_ref[...] * jnp.concatenate([corr] * (kv_rank // LANES), axis=1)
                        + _dot(p.astype(BF16), kc))
        m_ref[...] = m_new

    def body(kb, carry):
        step(kb, False)
        return carry

    lax.fori_loop(0, n_full, body, 0)
    step(n_full, True)
    o = (acc_ref[...] / jnp.sum(l_ref[...], axis=1, keepdims=True)).astype(BF16)
    for h in range(heads):
        o_ref[:, h * vdim:(h + 1) * vdim] = _dot(o[h * tq:(h + 1) * tq], wuv_ref[h]).astype(BF16)


def _sample_seq(b, nb, pt_ref, qlat_ref, qpe_ref, cnew_ref, knew_ref, ckv_hbm, kpe_hbm, o_ref,
                kbuf, pbuf, sem_k, sem_p, *, cpp, nbuf, n_chunks):
    total = nb * n_chunks
    page, kv_rank = kbuf.shape[2], kbuf.shape[3]
    ahead = nbuf - 2

    def copies(g, slot):
        out = []
        for j in range(cpp):
            pid = pt_ref[g * cpp + j]
            out.append(pltpu.make_async_copy(ckv_hbm.at[pid], kbuf.at[slot, j], sem_k.at[slot, j]))
            out.append(pltpu.make_async_copy(kpe_hbm.at[pid], pbuf.at[slot, j], sem_p.at[slot, j]))
        return out

    @pl.when(b == 0)
    def _():
        for g in range(ahead):
            for cp in copies(g, g % nbuf):
                cp.start()

    q = qlat_ref[...]
    qp = qpe_ref[...]
    heads = q.shape[0]
    m = jnp.full((heads, 1), NEG, F32)
    l = jnp.zeros((heads, 1), F32)
    acc = jnp.zeros((heads, kv_rank), F32)

    def fetch(c):
        g = b * n_chunks + c
        slot = lax.rem(g, nbuf)
        for cp in copies(g, slot):
            cp.wait()
        for cp in copies(jnp.minimum(g + ahead, total - 1), lax.rem(g + ahead, nbuf)):
            cp.start()
        kc = kbuf[slot].reshape(cpp * page, kv_rank).astype(BF16)
        kp = jnp.concatenate([pbuf[slot, j] for j in range(cpp)], axis=1).astype(BF16)
        return kc, _dot_nt(q, kc) + _dot(qp, kp)

    kc, s = fetch(0)
    for c in range(n_chunks):
        if c + 1 < n_chunks:
            kc_next, s_next = fetch(c + 1)
        m_new = jnp.maximum(m, jnp.max(s, axis=1, keepdims=True))
        corr = jnp.exp(m - m_new)
        p = jnp.exp(s - m_new)
        l = l * corr + jnp.sum(p, axis=1, keepdims=True)
        acc = acc * corr + _dot(p.astype(BF16), kc)
        m = m_new
        if c + 1 < n_chunks:
            kc, s = kc_next, s_next

    @pl.when(b == nb - 1)
    def _():
        for k in range(ahead):
            for cp in copies(total - 1, lax.rem(total + k, nbuf)):
                cp.wait()

    cn = cnew_ref[...]
    s_new = (jnp.sum(q.astype(F32) * cn, axis=1, keepdims=True)
             + jnp.sum(qp.astype(F32) * knew_ref[...], axis=1, keepdims=True))
    m_tot = jnp.maximum(m, s_new)
    c_old = jnp.exp(m - m_tot)
    p_new = jnp.exp(s_new - m_tot)
    o_ref[...] = (acc * c_old + p_new * cn) / (l * c_old + p_new)


def _attn_kernel(pt_ref, sq_ref, sqp_ref, cnew_ref, knew_ref, pq_ref, pqp_ref, kc_ref, kp_ref,
                 wuv_ref, *rest, n_cast, nq, split, tq, tk, cpp, nbuf, n_chunks):
    cast_in = rest[:n_cast]
    ckv_hbm, kpe_hbm, so_ref, po_ref = rest[n_cast:n_cast + 4]
    cast_out = rest[n_cast + 4:2 * n_cast + 4]
    kbuf, pbuf, sem_k, sem_p, m_ref, l_ref, acc_ref = rest[2 * n_cast + 4:]
    b = pl.program_id(0)
    for src, dst in zip(cast_in, cast_out):
        dst[...] = src[...].astype(BF16)
    _prompt_unit(lax.rem(b // split, nq), pq_ref, pqp_ref, kc_ref, kp_ref, wuv_ref, po_ref,
                 m_ref, l_ref, acc_ref, tq=tq, tk=tk)
    _sample_seq(b, pl.num_programs(0), pt_ref, sq_ref, sqp_ref, cnew_ref, knew_ref, ckv_hbm,
                kpe_hbm, so_ref, kbuf, pbuf, sem_k, sem_p, cpp=cpp, nbuf=nbuf, n_chunks=n_chunks)


def _slab_spec(arr, steps):
    r, c = arr.shape
    n_slabs = steps
    while r % n_slabs or (r // n_slabs) % BF16_SUBLANES:
        n_slabs //= 2
    assert n_slabs >= 1
    return pl.BlockSpec((r // n_slabs, c), lambda b, pt: (jnp.minimum(b, n_slabs - 1), 0))


def _attention(sq, sqp, cnew, knew, cache_ckv, cache_kpe_t, page_table, pq, pqp, kc, kp, wuv,
               to_cast, *, batch, seq, tq, tk, cpp, nbuf):
    bd, heads, kv_rank = sq.shape
    rope_dim = sqp.shape[-1]
    n = pq.shape[1]
    vdim = wuv.shape[-1]
    nq = seq // tq
    split = bd // (batch * nq)
    assert split * batch * nq == bd and heads % split == 0
    hpg = heads // split
    assert tk % tq == 0 and seq % tk == 0
    n_pages = page_table.shape[1]
    page = cache_ckv.shape[1]
    n_chunks = n_pages // cpp
    assert n_pages % cpp == 0 and nbuf >= 3
    pt = page_table.reshape(-1)
    kern = functools.partial(_attn_kernel, n_cast=len(to_cast), nq=nq, split=split, tq=tq, tk=tk,
                             cpp=cpp, nbuf=nbuf, n_chunks=n_chunks)
    per_b = lambda r, width: pl.BlockSpec((None, r, width), lambda b, pt_ref: (b, 0, 0))
    slabs = [_slab_spec(a, bd) for a in to_cast]
    any_spec = pl.BlockSpec(memory_space=pl.ANY)
    grid_spec = pltpu.PrefetchScalarGridSpec(
        num_scalar_prefetch=1,
        grid=(bd,),
        in_specs=[per_b(heads, kv_rank), per_b(heads, rope_dim), per_b(1, kv_rank),
                  per_b(1, rope_dim),
                  pl.BlockSpec((hpg, tq, kv_rank), lambda b, pt_ref: (b % split, b // split, 0)),
                  pl.BlockSpec((hpg, tq, rope_dim), lambda b, pt_ref: (b % split, b // split, 0)),
                  pl.BlockSpec((seq, kv_rank), lambda b, pt_ref: (b // (split * nq), 0)),
                  pl.BlockSpec((seq, rope_dim), lambda b, pt_ref: (b // (split * nq), 0)),
                  pl.BlockSpec((hpg,) + wuv.shape[1:], lambda b, pt_ref: (b % split, 0, 0))]
                 + slabs + [any_spec, any_spec],
        out_specs=[per_b(heads, kv_rank),
                   pl.BlockSpec((tq, hpg * vdim), lambda b, pt_ref: (b // split, b % split))]
                  + slabs,
        scratch_shapes=[pltpu.VMEM((nbuf, cpp, page, kv_rank), F32),
                        pltpu.VMEM((nbuf, cpp, rope_dim, page), F32),
                        pltpu.SemaphoreType.DMA((nbuf, cpp)), pltpu.SemaphoreType.DMA((nbuf, cpp)),
                        pltpu.VMEM((hpg * tq, LANES), F32), pltpu.VMEM((hpg * tq, LANES), F32),
                        pltpu.VMEM((hpg * tq, kv_rank), F32)],
    )
    outs = pl.pallas_call(
        kern,
        grid_spec=grid_spec,
        out_shape=[jax.ShapeDtypeStruct((bd, heads, kv_rank), F32),
                   jax.ShapeDtypeStruct((n, heads * vdim), BF16)]
                  + [jax.ShapeDtypeStruct(a.shape, BF16) for a in to_cast],
        compiler_params=_params("arbitrary"),
        name="attention",
    )(pt, sq, sqp, cnew, knew, pq, pqp, kc, kp, wuv, *to_cast, cache_ckv, cache_kpe_t)
    return outs[0], outs[1], outs[2:]


def _uv_kernel(o_ref, wuv_ref, out_ref, *, heads):
    kv_rank = wuv_ref.shape[1]
    vdim = wuv_ref.shape[2]
    for h in range(heads):
        oh = o_ref[:, h * kv_rank:(h + 1) * kv_rank].astype(BF16)
        out_ref[:, h * vdim:(h + 1) * vdim] = _dot(oh, wuv_ref[h]).astype(BF16)


def _uv_proj(o_lat, wuv):
    n = o_lat.shape[0]
    heads, kv_rank, vdim = wuv.shape
    return pl.pallas_call(
        functools.partial(_uv_kernel, heads=heads),
        grid=(1,),
        in_specs=[_const_spec(o_lat.shape), _const_spec(wuv.shape)],
        out_specs=_const_spec((n, heads * vdim)),
        out_shape=jax.ShapeDtypeStruct((n, heads * vdim), BF16),
        compiler_params=_params("arbitrary"),
        name="uv_proj",
    )(o_lat, wuv)


def _out_proj_kernel(x_ref, oa_ref, ocm_ref, wo_ref, gffn_ref, h_ref, hf_ref):
    wa = oa_ref.shape[1]
    h1 = x_ref[...] + _dot(oa_ref[...], wo_ref[:wa, :]) + _dot(ocm_ref[...], wo_ref[wa:, :])
    h_ref[...] = h1
    hf_ref[...] = _rms(h1, gffn_ref[...]).astype(BF16)


def _out_proj(x, oa, ocm, w, *, bm):
    n, d = x.shape
    row = lambda width: pl.BlockSpec((bm, width), lambda i: (i, 0))
    return pl.pallas_call(
        _out_proj_kernel,
        grid=(n // bm,),
        in_specs=[row(d), row(oa.shape[1]), row(ocm.shape[1]), _const_spec(w["w_o"].shape),
                  _const_spec(w["g_ffn"].shape)],
        out_specs=(row(d), row(d)),
        out_shape=(jax.ShapeDtypeStruct((n, d), F32), jax.ShapeDtypeStruct((n, d), BF16)),
        compiler_params=_params("parallel"),
        name="out_proj",
    )(x, oa, ocm, w["w_o"], w["g_ffn"])


def _ffn_kernel(hf_ref, wg_ref, wu_ref, wd_ref, out_ref):
    j = pl.program_id(1)

    @pl.when(j == 0)
    def _():
        out_ref[...] = jnp.zeros(out_ref.shape, F32)

    hf = hf_ref[...]
    t = (jax.nn.silu(_dot(hf, wg_ref[...])) * _dot(hf, wu_ref[...])).astype(BF16)
    out_ref[...] += _dot(t, wd_ref[...])


def _ffn(hf, w, *, bm, bh):
    n, d = hf.shape
    hidden = w["w_gate"].shape[1]
    assert hidden % bh == 0 and n % bm == 0
    return pl.pallas_call(
        _ffn_kernel,
        grid=(n // bm, hidden // bh),
        in_specs=[pl.BlockSpec((bm, d), lambda i, j: (i, 0)),
                  pl.BlockSpec((d, bh), lambda i, j: (0, j)),
                  pl.BlockSpec((d, bh), lambda i, j: (0, j)),
                  pl.BlockSpec((bh, d), lambda i, j: (j, 0))],
        out_specs=pl.BlockSpec((bm, d), lambda i, j: (i, 0)),
        out_shape=jax.ShapeDtypeStruct((n, d), F32),
        compiler_params=_params("parallel", "arbitrary"),
        name="ffn",
    )(hf, w["w_gate"], w["w_up"], w["w_down"])


def _ple_kernel(h_ref, f_ref, p_ref, gple_ref, wg_ref, wp_ref, gfin_ref, y_ref, *, final):
    h = h_ref[...] + f_ref[...]
    gate = jax.nn.sigmoid(_dot(_rms(h, gple_ref[...]).astype(BF16), wg_ref[...]))
    h3 = h + gate * _dot(p_ref[...].astype(BF16), wp_ref[...])
    y_ref[...] = _rms(h3, gfin_ref[...]) if final else h3


def _ple(h, f, p, w, g_final, *, bm, final):
    n, d = h.shape
    row = lambda width: pl.BlockSpec((bm, width), lambda i: (i, 0))
    return pl.pallas_call(
        functools.partial(_ple_kernel, final=final),
        grid=(n // bm,),
        in_specs=[row(d), row(d), row(p.shape[1]), _const_spec(w["g_ple"].shape),
                  _const_spec(w["w_ple_gate"].shape), _const_spec(w["w_ple_proj"].shape),
                  _const_spec(g_final.shape)],
        out_specs=row(d),
        out_shape=jax.ShapeDtypeStruct((n, d), F32),
        compiler_params=_params("parallel"),
        name="ple",
    )(h, f, p, w["g_ple"], w["w_ple_gate"], w["w_ple_proj"], g_final)


def _prep_weights(i, g_mix, w_in, g_q, w_uq, g_kv, w_uk, w_uv, g_v, w_s, b_s, w_o, g_ffn,
                  w_ffn_gate, w_ffn_up, w_ffn_down, g_ple, w_ple_gate, w_ple_proj):
    q_rank = g_q.shape[1]
    kv_rank = g_kv.shape[1]
    heads, qk_dim = w_uq.shape[2], w_uq.shape[3]
    nope = w_uk.shape[3]
    rope_dim = qk_dim - nope
    groups = w_s.shape[1]
    cm_width = g_v.shape[1]
    wi = w_in[i]
    pe = wi[:, q_rank + kv_rank:q_rank + kv_rank + rope_dim]
    half = rope_dim // 2
    inv = ROPE_THETA ** (-jnp.arange(half, dtype=F32) / half)
    lane = jnp.arange(LANES)
    return {
        "g_mix": g_mix[i][None], "g_q": g_q[i][None], "g_kv": g_kv[i][None], "g_v": g_v[i][None],
        "g_ffn": g_ffn[i][None], "g_ple": g_ple[i][None],
        "w_a": wi[:, :q_rank + kv_rank].astype(BF16),
        "w_pe": jnp.concatenate([pe] * (LANES // rope_dim), axis=1).astype(BF16),
        "w_cm": wi[:, q_rank + kv_rank + rope_dim:].astype(BF16),
        "inv128": jnp.tile(inv, LANES // half)[None],
        "sign128": jnp.where((lane % rope_dim) < half, -1.0, 1.0).astype(F32)[None],
        "w_s": w_s[i], "b_sT": b_s[i].T,
        "w_s_diag": jnp.repeat(w_s[i][:, 0, 0], cm_width // groups)[None],
        "b_s_diag": jnp.repeat(b_s[i][:, 0], cm_width // groups)[None],
        "w_uq": jnp.concatenate([w_uq[i][:, :, :nope].reshape(q_rank, heads * nope),
                                 w_uq[i][:, :, nope:].reshape(q_rank, heads * rope_dim)],
                                axis=1).astype(BF16),
        "w_ukT": jnp.transpose(w_uk[i], (1, 2, 0)).astype(BF16),
        "w_uvT": jnp.transpose(w_uv[i], (1, 0, 2)).astype(BF16),
        "w_ple_proj": w_ple_proj[i].astype(BF16),
        "w_o": w_o[i], "w_gate": w_ffn_gate[i], "w_up": w_ffn_up[i], "w_down": w_ffn_down[i],
        "w_ple_gate": w_ple_gate[i],
    }


LATE_WEIGHTS = ("w_o", "w_gate", "w_up", "w_down", "w_ple_gate")


def kernel(x_prompt, x_sample, p_prompt, p_sample, cache_ckv, cache_kpe, page_table, g_mix, w_in,
           g_q, w_uq, g_kv, w_uk, w_uv, g_v, w_s, b_s, w_o, g_ffn, w_ffn_gate, w_ffn_up,
           w_ffn_down, g_ple, w_ple_gate, w_ple_proj, g_final):
    batch, seq, d_model = x_prompt.shape
    dec_batch, dec_seq, _ = x_sample.shape
    depth = g_mix.shape[0]
    q_rank, kv_rank = g_q.shape[1], g_kv.shape[1]
    heads, qk_dim = w_uq.shape[2], w_uq.shape[3]
    nope = w_uk.shape[3]
    rope_dim = qk_dim - nope
    cm_width = g_v.shape[1]
    groups, chunk = w_s.shape[1], w_s.shape[2]
    page = cache_ckv.shape[2]
    past_len = page_table.shape[1] * page
    scale = 1.0 / math.sqrt(nope + rope_dim)
    assert dec_seq == 1 and seq % chunk == 0
    in_dims = (q_rank, kv_rank, rope_dim, cm_width, chunk, groups)
    q_dims = (heads, nope, rope_dim, kv_rank)
    gfin = g_final[None]

    hp = x_prompt.reshape(batch * seq, d_model)
    hs = x_sample.reshape(dec_batch * dec_seq, d_model)
    outs = [[] for _ in range(6)]
    for i in range(depth):
        w = _prep_weights(i, g_mix, w_in, g_q, w_uq, g_kv, w_uk, w_uv, g_v, w_s, b_s, w_o, g_ffn,
                          w_ffn_gate, w_ffn_up, w_ffn_down, g_ple, w_ple_gate, w_ple_proj)

        cq, ckv_p, kpe_p, ckvb, kpeb, ocm_p, vn_p, cos, sin = _in_proj(
            hp, w, bm=512, seq_len=seq, pos0=0, dims=in_dims)
        qlat_p, qpe_p = _q_proj(cq, cos, sin, w, bm=512, dims=q_dims, scale=scale)
        cq, ckv_s, kpe_s, _, _, ocm_s, vn_s, cos, sin = _in_proj(
            hs, w, bm=dec_batch, seq_len=dec_seq, pos0=past_len, dims=in_dims)
        qlat_s, qpe_s = _q_proj(cq, cos, sin, w, bm=dec_batch, dims=q_dims, scale=scale)
        o_lat, oa_p, late = _attention(
            jnp.transpose(qlat_s, (1, 0, 2)), jnp.transpose(qpe_s, (1, 0, 2)),
            ckv_s[:, None, :], kpe_s[:, None, :], cache_ckv[i], jnp.swapaxes(cache_kpe[i], 1, 2),
            page_table, qlat_p, qpe_p, ckvb, kpeb, w["w_uvT"], [w[k] for k in LATE_WEIGHTS],
            batch=batch, seq=seq, tq=128, tk=512, cpp=16, nbuf=6)
        w.update(zip(LATE_WEIGHTS, late))

        h1, hf = _out_proj(hp, oa_p, ocm_p, w, bm=512)
        f = _ffn(hf, w, bm=1024, bh=512)
        hp = _ple(h1, f, p_prompt[i].reshape(batch * seq, -1), w, gfin, bm=512,
                  final=i == depth - 1)
        outs[0].append(ckv_p.reshape(batch, seq, kv_rank))
        outs[1].append(kpe_p.reshape(batch, seq, rope_dim))
        outs[2].append(vn_p.reshape(batch, seq, cm_width)[:, seq - chunk:])

        oa_s = _uv_proj(o_lat.reshape(dec_batch, heads * kv_rank), w["w_uvT"])
        h1, hf = _out_proj(hs, oa_s, ocm_s, w, bm=dec_batch)
        f = _ffn(hf, w, bm=dec_batch, bh=512)
        hs = _ple(h1, f, p_sample[i].reshape(dec_batch * dec_seq, -1), w, gfin, bm=dec_batch,
                  final=i == depth - 1)
        outs[3].append(ckv_s.reshape(dec_batch, dec_seq, kv_rank))
        outs[4].append(kpe_s.reshape(dec_batch, dec_seq, rope_dim))
        outs[5].append(vn_s.reshape(dec_batch, dec_seq, cm_width))

    y_prompt = hp.reshape(batch, seq, d_model)
    y_sample = hs.reshape(dec_batch, dec_seq, d_model)
    return (y_prompt, y_sample) + tuple(jnp.stack(o) for o in outs)
```

```python
import functools
import math

import jax
import jax.numpy as jnp
from jax import lax
from jax.experimental import pallas as pl
from jax.experimental.pallas import tpu as pltpu

F32 = jnp.float32
BF16 = jnp.bfloat16

EPS = 1e-6
NEG = -1e30
ROPE_THETA = 10000.0
LANES = 128
SUBLANES = 8
BF16_SUBLANES = 16
VMEM_LIMIT = 56 * 1024 * 1024


def _params(*sem):
    return pltpu.CompilerParams(dimension_semantics=sem, vmem_limit_bytes=VMEM_LIMIT)


def _rms(x, g):
    return x * lax.rsqrt(jnp.mean(x * x, axis=-1, keepdims=True) + EPS) * g


def _dot(a, b):
    return jnp.dot(a, b, preferred_element_type=F32)


def _dot_nt(a, b):
    return lax.dot_general(a, b, (((1,), (1,)), ((), ())), preferred_element_type=F32)


def _const_spec(shape):
    nd = len(shape)
    return pl.BlockSpec(shape, lambda *_: (0,) * nd)


def _in_proj_kernel(x_ref, gmix_ref, wa_ref, wpe_ref, wcm_ref, gq_ref, gkv_ref, gv_ref,
                    inv_ref, sign_ref, ws_ref, bs_ref,
                    cq_ref, ckv_ref, kpe_ref, ckvb_ref, kpeb_ref, ocm_ref, vn_ref,
                    cos_ref, sin_ref, *, bm, seq_len, pos0, q_rank, rope_dim, cm_width,
                    chunk, groups):
    i = pl.program_id(0)
    hn = _rms(x_ref[...], gmix_ref[...]).astype(BF16)

    za = _dot_nt(hn, wa_ref[...])
    cq_ref[...] = _rms(za[:, :q_rank], gq_ref[...]).astype(BF16)
    ckv = _rms(za[:, q_rank:], gkv_ref[...])
    ckv_ref[...] = ckv
    ckvb_ref[...] = ckv.astype(BF16)

    row = i * bm + lax.broadcasted_iota(jnp.int32, (bm, LANES), 0)
    pos = pos0 + lax.rem(row, seq_len)
    ang = pos.astype(F32) * inv_ref[...]
    c = jnp.cos(ang)
    s = jnp.sin(ang) * sign_ref[...]
    cos_ref[...] = c
    sin_ref[...] = s
    zpe = _dot_nt(hn, wpe_ref[...])
    kpe = (zpe * c + pltpu.roll(zpe, rope_dim // 2, axis=1) * s)[:, :rope_dim]
    kpe_ref[...] = kpe
    kpeb_ref[...] = kpe.astype(BF16)

    gz = jax.nn.gelu(_dot_nt(hn, wcm_ref[...]))
    u = gz[:, :cm_width]
    vn = _rms(gz[:, cm_width:], gv_ref[...])
    vn_ref[...] = vn
    if seq_len % chunk == 0:
        vb = vn.astype(BF16)
        r = lax.broadcasted_iota(jnp.int32, (chunk, chunk), 0)
        cc = lax.broadcasted_iota(jnp.int32, (chunk, chunk), 1)
        head = cm_width // groups
        for g in range(groups):
            wsg = jnp.where(r >= cc, ws_ref[g], 0.0).astype(BF16)
            bcol = bs_ref[:, g:g + 1]
            for k in range(bm // chunk):
                rows = slice(k * chunk, (k + 1) * chunk)
                cols = slice(g * head, (g + 1) * head)
                mixed = _dot(wsg, vb[rows, cols]) + bcol
                ocm_ref[rows, cols] = (u[rows, cols] * mixed).astype(BF16)
    else:
        ocm_ref[...] = (u * (vn * ws_ref[...] + bs_ref[...])).astype(BF16)


def _in_proj(x, w, *, bm, seq_len, pos0, dims):
    n, d = x.shape
    q_rank, kv_rank, rope_dim, cm_width, chunk, groups = dims
    chunked = seq_len % chunk == 0
    assert chunked or seq_len == 1
    if chunked:
        ws, bs = w["w_s"], w["b_sT"]
    else:
        ws, bs = w["w_s_diag"], w["b_s_diag"]
    kern = functools.partial(_in_proj_kernel, bm=bm, seq_len=seq_len, pos0=pos0, q_rank=q_rank,
                             rope_dim=rope_dim, cm_width=cm_width, chunk=chunk, groups=groups)
    row = lambda width: pl.BlockSpec((bm, width), lambda i: (i, 0))
    consts = [w["g_mix"], w["w_a"], w["w_pe"], w["w_cm"], w["g_q"], w["g_kv"], w["g_v"],
              w["inv128"], w["sign128"], ws, bs]
    out_shape = (
        jax.ShapeDtypeStruct((n, q_rank), BF16),
        jax.ShapeDtypeStruct((n, kv_rank), F32),
        jax.ShapeDtypeStruct((n, rope_dim), F32),
        jax.ShapeDtypeStruct((n, kv_rank), BF16),
        jax.ShapeDtypeStruct((n, rope_dim), BF16),
        jax.ShapeDtypeStruct((n, cm_width), BF16),
        jax.ShapeDtypeStruct((n, cm_width), F32),
        jax.ShapeDtypeStruct((n, LANES), F32),
        jax.ShapeDtypeStruct((n, LANES), F32),
    )
    return pl.pallas_call(
        kern,
        grid=(n // bm,),
        in_specs=[row(d)] + [_const_spec(a.shape) for a in consts],
        out_specs=tuple(row(s.shape[1]) for s in out_shape),
        out_shape=out_shape,
        compiler_params=_params("parallel"),
        name="in_proj",
    )(x, *consts)


def _q_proj_kernel(cq_ref, wuq_ref, wuk_ref, cos_ref, sin_ref, qlat_ref, qpe_ref, *,
                   heads, nope, rope_dim, scale):
    q = _dot(cq_ref[...], wuq_ref[...])
    qn = q[:, :heads * nope].astype(BF16)
    for h in range(heads):
        qlat_ref[h] = (_dot(qn[:, h * nope:(h + 1) * nope], wuk_ref[h]) * scale).astype(BF16)
    qp = q[:, heads * nope:]
    width = heads * rope_dim
    reps = width // LANES
    c = jnp.concatenate([cos_ref[...]] * reps, axis=1)
    s = jnp.concatenate([sin_ref[...]] * reps, axis=1)
    half = rope_dim // 2
    lane = lax.broadcasted_iota(jnp.int32, qp.shape, 1)
    first_half = lax.rem(lane, rope_dim) < half
    partner = jnp.where(first_half, pltpu.roll(qp, width - half, axis=1),
                        pltpu.roll(qp, half, axis=1))
    qr = (qp * c + partner * s) * scale
    for h in range(heads):
        qpe_ref[h] = qr[:, h * rope_dim:(h + 1) * rope_dim].astype(BF16)


def _q_proj(cq, cos, sin, w, *, bm, dims, scale):
    n, q_rank = cq.shape
    heads, nope, rope_dim, kv_rank = dims
    kern = functools.partial(_q_proj_kernel, heads=heads, nope=nope, rope_dim=rope_dim,
                             scale=scale)
    row = lambda width: pl.BlockSpec((bm, width), lambda i: (i, 0))
    return pl.pallas_call(
        kern,
        grid=(n // bm,),
        in_specs=[row(q_rank), _const_spec(w["w_uq"].shape), _const_spec(w["w_ukT"].shape),
                  row(LANES), row(LANES)],
        out_specs=(pl.BlockSpec((heads, bm, kv_rank), lambda i: (0, i, 0)),
                   pl.BlockSpec((heads, bm, rope_dim), lambda i: (0, i, 0))),
        out_shape=(jax.ShapeDtypeStruct((heads, n, kv_rank), BF16),
                   jax.ShapeDtypeStruct((heads, n, rope_dim), BF16)),
        compiler_params=_params("parallel"),
        name="q_proj",
    )(cq, w["w_uq"], w["w_ukT"], cos, sin)


def _prompt_unit(qi, qlat_ref, qpe_ref, kc_ref, kp_ref, wuv_ref, o_ref, m_ref, l_ref, acc_ref,
                 *, tq, tk):
    heads, _, kv_rank = qlat_ref.shape
    rope_dim = qpe_ref.shape[-1]
    vdim = wuv_ref.shape[-1]
    rows = heads * tq
    q = qlat_ref[...].reshape(rows, kv_rank)
    qp = qpe_ref[...].reshape(rows, rope_dim)
    m_ref[...] = jnp.full(m_ref.shape, NEG, F32)
    l_ref[...] = jnp.zeros(l_ref.shape, F32)
    acc_ref[...] = jnp.zeros(acc_ref.shape, F32)
    n_full = (qi * tq) // tk

    def step(kb, masked):
        k0 = pl.multiple_of(kb * tk, tk)
        kc = kc_ref[pl.ds(k0, tk), :]
        kp = kp_ref[pl.ds(k0, tk), :]
        s = _dot_nt(q, kc) + _dot_nt(qp, kp)
        if masked:
            qpos = qi * tq + lax.rem(lax.broadcasted_iota(jnp.int32, (rows, tk), 0), tq)
            col = lax.broadcasted_iota(jnp.int32, (rows, tk), 1)
            s = jnp.where(k0 + col <= qpos, s, NEG)
        m_old = m_ref[...]
        m_new = jnp.maximum(m_old, jnp.max(s, axis=1, keepdims=True))
        corr = jnp.exp(m_old - m_new)
        p = jnp.exp(s - jnp.concatenate([m_new] * (tk // LANES), axis=1))
        psum = p[:, :LANES]
        for t in range(1, tk // LANES):
            psum = psum + p[:, t * LANES:(t + 1) * LANES]
        l_ref[...] = l_ref[...] * corr + psum
        acc_ref[...] = (acc_ref[...] * jnp.concatenate([corr] * (kv_rank // LANES), axis=1)
                        + _dot(p.astype(BF16), kc))
        m_ref[...] = m_new

    def body(kb, carry):
        step(kb, False)
        return carry

    lax.fori_loop(0, n_full, body, 0)
    step(n_full, True)
    o = (acc_ref[...] / jnp.sum(l_ref[...], axis=1, keepdims=True)).astype(BF16)
    for h in range(heads):
        o_ref[:, h * vdim:(h + 1) * vdim] = _dot(o[h * tq:(h + 1) * tq], wuv_ref[h]).astype(BF16)


def _sample_seq(b, nb, pt_ref, qlat_ref, qpe_ref, cnew_ref, knew_ref, ckv_hbm, kpe_hbm, o_ref,
                kbuf, pbuf, sem_k, sem_p, *, cpp, nbuf, n_chunks):
    total = nb * n_chunks
    page, kv_rank = kbuf.shape[2], kbuf.shape[3]
    ahead = nbuf - 2

    def copies(g, slot):
        out = []
        for j in range(cpp):
            pid = pt_ref[g * cpp + j]
            out.append(pltpu.make_async_copy(ckv_hbm.at[pid], kbuf.at[slot, j], sem_k.at[slot, j]))
            out.append(pltpu.make_async_copy(kpe_hbm.at[pid], pbuf.at[slot, j], sem_p.at[slot, j]))
        return out

    @pl.when(b == 0)
    def _():
        for g in range(ahead):
            for cp in copies(g, g % nbuf):
                cp.start()

    q = qlat_ref[...]
    qp = qpe_ref[...]
    heads = q.shape[0]
    m = jnp.full((heads, 1), NEG, F32)
    l = jnp.zeros((heads, 1), F32)
    acc = jnp.zeros((heads, kv_rank), F32)

    def fetch(c):
        g = b * n_chunks + c
        slot = lax.rem(g, nbuf)
        for cp in copies(g, slot):
            cp.wait()
        for cp in copies(jnp.minimum(g + ahead, total - 1), lax.rem(g + ahead, nbuf)):
            cp.start()
        kc = kbuf[slot].reshape(cpp * page, kv_rank).astype(BF16)
        kp = jnp.concatenate([pbuf[slot, j] for j in range(cpp)], axis=1).astype(BF16)
        return kc, _dot_nt(q, kc) + _dot(qp, kp)

    kc, s = fetch(0)
    for c in range(n_chunks):
        if c + 1 < n_chunks:
            kc_next, s_next = fetch(c + 1)
        m_new = jnp.maximum(m, jnp.max(s, axis=1, keepdims=True))
        corr = jnp.exp(m - m_new)
        p = jnp.exp(s - m_new)
        l = l * corr + jnp.sum(p, axis=1, keepdims=True)
        acc = acc * corr + _dot(p.astype(BF16), kc)
        m = m_new
        if c + 1 < n_chunks:
            kc, s = kc_next, s_next

    @pl.when(b == nb - 1)
    def _():
        for k in range(ahead):
            for cp in copies(total - 1, lax.rem(total + k, nbuf)):
                cp.wait()

    cn = cnew_ref[...]
    s_new = (jnp.sum(q.astype(F32) * cn, axis=1, keepdims=True)
             + jnp.sum(qp.astype(F32) * knew_ref[...], axis=1, keepdims=True))
    m_tot = jnp.maximum(m, s_new)
    c_old = jnp.exp(m - m_tot)
    p_new = jnp.exp(s_new - m_tot)
    o_ref[...] = (acc * c_old + p_new * cn) / (l * c_old + p_new)


def _attn_kernel(pt_ref, sq_ref, sqp_ref, cnew_ref, knew_ref, pq_ref, pqp_ref, kc_ref, kp_ref,
                 wuv_ref, *rest, n_cast, nq, split, tq, tk, cpp, nbuf, n_chunks):
    cast_in = rest[:n_cast]
    ckv_hbm, kpe_hbm, so_ref, po_ref = rest[n_cast:n_cast + 4]
    cast_out = rest[n_cast + 4:2 * n_cast + 4]
    kbuf, pbuf, sem_k, sem_p, m_ref, l_ref, acc_ref = rest[2 * n_cast + 4:]
    b = pl.program_id(0)
    for src, dst in zip(cast_in, cast_out):
        dst[...] = src[...].astype(BF16)
    _prompt_unit(lax.rem(b // split, nq), pq_ref, pqp_ref, kc_ref, kp_ref, wuv_ref, po_ref,
                 m_ref, l_ref, acc_ref, tq=tq, tk=tk)
    _sample_seq(b, pl.num_programs(0), pt_ref, sq_ref, sqp_ref, cnew_ref, knew_ref, ckv_hbm,
                kpe_hbm, so_ref, kbuf, pbuf, sem_k, sem_p, cpp=cpp, nbuf=nbuf, n_chunks=n_chunks)


def _slab_spec(arr, steps):
    r, c = arr.shape
    n_slabs = steps
    while r % n_slabs or (r // n_slabs) % BF16_SUBLANES:
        n_slabs //= 2
    assert n_slabs >= 1
    return pl.BlockSpec((r // n_slabs, c), lambda b, pt: (jnp.minimum(b, n_slabs - 1), 0))


def _attention(sq, sqp, cnew, knew, cache_ckv, cache_kpe_t, page_table, pq, pqp, kc, kp, wuv,
               to_cast, *, batch, seq, tq, tk, cpp, nbuf):
    bd, heads, kv_rank = sq.shape
    rope_dim = sqp.shape[-1]
    n = pq.shape[1]
    vdim = wuv.shape[-1]
    nq = seq // tq
    split = bd // (batch * nq)
    assert split * batch * nq == bd and heads % split == 0
    hpg = heads // split
    assert tk % tq == 0 and seq % tk == 0
    n_pages = page_table.shape[1]
    page = cache_ckv.shape[1]
    n_chunks = n_pages // cpp
    assert n_pages % cpp == 0 and nbuf >= 3
    pt = page_table.reshape(-1)
    kern = functools.partial(_attn_kernel, n_cast=len(to_cast), nq=nq, split=split, tq=tq, tk=tk,
                             cpp=cpp, nbuf=nbuf, n_chunks=n_chunks)
    per_b = lambda r, width: pl.BlockSpec((None, r, width), lambda b, pt_ref: (b, 0, 0))
    slabs = [_slab_spec(a, bd) for a in to_cast]
    any_spec = pl.BlockSpec(memory_space=pl.ANY)
    grid_spec = pltpu.PrefetchScalarGridSpec(
        num_scalar_prefetch=1,
        grid=(bd,),
        in_specs=[per_b(heads, kv_rank), per_b(heads, rope_dim), per_b(1, kv_rank),
                  per_b(1, rope_dim),
                  pl.BlockSpec((hpg, tq, kv_rank), lambda b, pt_ref: (b % split, b // split, 0)),
                  pl.BlockSpec((hpg, tq, rope_dim), lambda b, pt_ref: (b % split, b // split, 0)),
                  pl.BlockSpec((seq, kv_rank), lambda b, pt_ref: (b // (split * nq), 0)),
                  pl.BlockSpec((seq, rope_dim), lambda b, pt_ref: (b // (split * nq), 0)),
                  pl.BlockSpec((hpg,) + wuv.shape[1:], lambda b, pt_ref: (b % split, 0, 0))]
                 + slabs + [any_spec, any_spec],
        out_specs=[per_b(heads, kv_rank),
                   pl.BlockSpec((tq, hpg * vdim), lambda b, pt_ref: (b // split, b % split))]
                  + slabs,
        scratch_shapes=[pltpu.VMEM((nbuf, cpp, page, kv_rank), F32),
                        pltpu.VMEM((nbuf, cpp, rope_dim, page), F32),
                        pltpu.SemaphoreType.DMA((nbuf, cpp)), pltpu.SemaphoreType.DMA((nbuf, cpp)),
                        pltpu.VMEM((hpg * tq, LANES), F32), pltpu.VMEM((hpg * tq, LANES), F32),
                        pltpu.VMEM((hpg * tq, kv_rank), F32)],
    )
    outs = pl.pallas_call(
        kern,
        grid_spec=grid_spec,
        out_shape=[jax.ShapeDtypeStruct((bd, heads, kv_rank), F32),
                   jax.ShapeDtypeStruct((n, heads * vdim), BF16)]
                  + [jax.ShapeDtypeStruct(a.shape, BF16) for a in to_cast],
        compiler_params=_params("arbitrary"),
        name="attention",
    )(pt, sq, sqp, cnew, knew, pq, pqp, kc, kp, wuv, *to_cast, cache_ckv, cache_kpe_t)
    return outs[0], outs[1], outs[2:]


def _uv_kernel(o_ref, wuv_ref, out_ref, *, heads):
    kv_rank = wuv_ref.shape[1]
    vdim = wuv_ref.shape[2]
    for h in range(heads):
        oh = o_ref[:, h * kv_rank:(h + 1) * kv_rank].astype(BF16)
        out_ref[:, h * vdim:(h + 1) * vdim] = _dot(oh, wuv_ref[h]).astype(BF16)


def _uv_proj(o_lat, wuv):
    n = o_lat.shape[0]
    heads, kv_rank, vdim = wuv.shape
    return pl.pallas_call(
        functools.partial(_uv_kernel, heads=heads),
        grid=(1,),
        in_specs=[_const_spec(o_lat.shape), _const_spec(wuv.shape)],
        out_specs=_const_spec((n, heads * vdim)),
        out_shape=jax.ShapeDtypeStruct((n, heads * vdim), BF16),
        compiler_params=_params("arbitrary"),
        name="uv_proj",
    )(o_lat, wuv)


def _out_proj_kernel(x_ref, oa_ref, ocm_ref, wo_ref, gffn_ref, h_ref, hf_ref):
    wa = oa_ref.shape[1]
    h1 = x_ref[...] + _dot(oa_ref[...], wo_ref[:wa, :]) + _dot(ocm_ref[...], wo_ref[wa:, :])
    h_ref[...] = h1
    hf_ref[...] = _rms(h1, gffn_ref[...]).astype(BF16)


def _out_proj(x, oa, ocm, w, *, bm):
    n, d = x.shape
    row = lambda width: pl.BlockSpec((bm, width), lambda i: (i, 0))
    return pl.pallas_call(
        _out_proj_kernel,
        grid=(n // bm,),
        in_specs=[row(d), row(oa.shape[1]), row(ocm.shape[1]), _const_spec(w["w_o"].shape),
                  _const_spec(w["g_ffn"].shape)],
        out_specs=(row(d), row(d)),
        out_shape=(jax.ShapeDtypeStruct((n, d), F32), jax.ShapeDtypeStruct((n, d), BF16)),
        compiler_params=_params("parallel"),
        name="out_proj",
    )(x, oa, ocm, w["w_o"], w["g_ffn"])


def _ffn_kernel(hf_ref, wg_ref, wu_ref, wd_ref, out_ref):
    j = pl.program_id(1)

    @pl.when(j == 0)
    def _():
        out_ref[...] = jnp.zeros(out_ref.shape, F32)

    hf = hf_ref[...]
    t = (jax.nn.silu(_dot(hf, wg_ref[...])) * _dot(hf, wu_ref[...])).astype(BF16)
    out_ref[...] += _dot(t, wd_ref[...])


def _ffn(hf, w, *, bm, bh):
    n, d = hf.shape
    hidden = w["w_gate"].shape[1]
    assert hidden % bh == 0 and n % bm == 0
    return pl.pallas_call(
        _ffn_kernel,
        grid=(n // bm, hidden // bh),
        in_specs=[pl.BlockSpec((bm, d), lambda i, j: (i, 0)),
                  pl.BlockSpec((d, bh), lambda i, j: (0, j)),
                  pl.BlockSpec((d, bh), lambda i, j: (0, j)),
                  pl.BlockSpec((bh, d), lambda i, j: (j, 0))],
        out_specs=pl.BlockSpec((bm, d), lambda i, j: (i, 0)),
        out_shape=jax.ShapeDtypeStruct((n, d), F32),
        compiler_params=_params("parallel", "arbitrary"),
        name="ffn",
    )(hf, w["w_gate"], w["w_up"], w["w_down"])


def _ple_kernel(h_ref, f_ref, p_ref, gple_ref, wg_ref, wp_ref, gfin_ref, y_ref, *, final):
    h = h_ref[...] + f_ref[...]
    gate = jax.nn.sigmoid(_dot(_rms(h, gple_ref[...]).astype(BF16), wg_ref[...]))
    h3 = h + gate * _dot(p_ref[...].astype(BF16), wp_ref[...])
    y_ref[...] = _rms(h3, gfin_ref[...]) if final else h3


def _ple(h, f, p, w, g_final, *, bm, final):
    n, d = h.shape
    row = lambda width: pl.BlockSpec((bm, width), lambda i: (i, 0))
    return pl.pallas_call(
        functools.partial(_ple_kernel, final=final),
        grid=(n // bm,),
        in_specs=[row(d), row(d), row(p.shape[1]), _const_spec(w["g_ple"].shape),
                  _const_spec(w["w_ple_gate"].shape), _const_spec(w["w_ple_proj"].shape),
                  _const_spec(g_final.shape)],
        out_specs=row(d),
        out_shape=jax.ShapeDtypeStruct((n, d), F32),
        compiler_params=_params("parallel"),
        name="ple",
    )(h, f, p, w["g_ple"], w["w_ple_gate"], w["w_ple_proj"], g_final)


def _prep_weights(i, g_mix, w_in, g_q, w_uq, g_kv, w_uk, w_uv, g_v, w_s, b_s, w_o, g_ffn,
                  w_ffn_gate, w_ffn_up, w_ffn_down, g_ple, w_ple_gate, w_ple_proj):
    q_rank = g_q.shape[1]
    kv_rank = g_kv.shape[1]
    heads, qk_dim = w_uq.shape[2], w_uq.shape[3]
    nope = w_uk.shape[3]
    rope_dim = qk_dim - nope
    groups = w_s.shape[1]
    cm_width = g_v.shape[1]
    wi = w_in[i]
    pe = wi[:, q_rank + kv_rank:q_rank + kv_rank + rope_dim]
    half = rope_dim // 2
    inv = ROPE_THETA ** (-jnp.arange(half, dtype=F32) / half)
    lane = jnp.arange(LANES)
    return {
        "g_mix": g_mix[i][None], "g_q": g_q[i][None], "g_kv": g_kv[i][None], "g_v": g_v[i][None],
        "g_ffn": g_ffn[i][None], "g_ple": g_ple[i][None],
        "w_a": wi[:, :q_rank + kv_rank].astype(BF16).T,
        "w_pe": jnp.concatenate([pe] * (LANES // rope_dim), axis=1).astype(BF16).T,
        "w_cm": wi[:, q_rank + kv_rank + rope_dim:].astype(BF16).T,
        "inv128": jnp.tile(inv, LANES // half)[None],
        "sign128": jnp.where((lane % rope_dim) < half, -1.0, 1.0).astype(F32)[None],
        "w_s": w_s[i], "b_sT": b_s[i].T,
        "w_s_diag": jnp.repeat(w_s[i][:, 0, 0], cm_width // groups)[None],
        "b_s_diag": jnp.repeat(b_s[i][:, 0], cm_width // groups)[None],
        "w_uq": jnp.concatenate([w_uq[i][:, :, :nope].reshape(q_rank, heads * nope),
                                 w_uq[i][:, :, nope:].reshape(q_rank, heads * rope_dim)],
                                axis=1).astype(BF16),
        "w_ukT": jnp.transpose(w_uk[i], (1, 2, 0)).astype(BF16),
        "w_uvT": jnp.transpose(w_uv[i], (1, 0, 2)).astype(BF16),
        "w_ple_proj": w_ple_proj[i].astype(BF16),
        "w_o": w_o[i], "w_gate": w_ffn_gate[i], "w_up": w_ffn_up[i], "w_down": w_ffn_down[i],
        "w_ple_gate": w_ple_gate[i],
    }


LATE_WEIGHTS = ("w_o", "w_gate", "w_up", "w_down", "w_ple_gate")


def kernel(x_prompt, x_sample, p_prompt, p_sample, cache_ckv, cache_kpe, page_table, g_mix, w_in,
           g_q, w_uq, g_kv, w_uk, w_uv, g_v, w_s, b_s, w_o, g_ffn, w_ffn_gate, w_ffn_up,
           w_ffn_down, g_ple, w_ple_gate, w_ple_proj, g_final):
    batch, seq, d_model = x_prompt.shape
    dec_batch, dec_seq, _ = x_sample.shape
    depth = g_mix.shape[0]
    q_rank, kv_rank = g_q.shape[1], g_kv.shape[1]
    heads, qk_dim = w_uq.shape[2], w_uq.shape[3]
    nope = w_uk.shape[3]
    rope_dim = qk_dim - nope
    cm_width = g_v.shape[1]
    groups, chunk = w_s.shape[1], w_s.shape[2]
    page = cache_ckv.shape[2]
    past_len = page_table.shape[1] * page
    scale = 1.0 / math.sqrt(nope + rope_dim)
    assert dec_seq == 1 and seq % chunk == 0
    in_dims = (q_rank, kv_rank, rope_dim, cm_width, chunk, groups)
    q_dims = (heads, nope, rope_dim, kv_rank)
    gfin = g_final[None]

    hp = x_prompt.reshape(batch * seq, d_model)
    hs = x_sample.reshape(dec_batch * dec_seq, d_model)
    outs = [[] for _ in range(6)]
    for i in range(depth):
        w = _prep_weights(i, g_mix, w_in, g_q, w_uq, g_kv, w_uk, w_uv, g_v, w_s, b_s, w_o, g_ffn,
                          w_ffn_gate, w_ffn_up, w_ffn_down, g_ple, w_ple_gate, w_ple_proj)

        cq, ckv_p, kpe_p, ckvb, kpeb, ocm_p, vn_p, cos, sin = _in_proj(
            hp, w, bm=512, seq_len=seq, pos0=0, dims=in_dims)
        qlat_p, qpe_p = _q_proj(cq, cos, sin, w, bm=512, dims=q_dims, scale=scale)
        cq, ckv_s, kpe_s, _, _, ocm_s, vn_s, cos, sin = _in_proj(
            hs, w, bm=dec_batch, seq_len=dec_seq, pos0=past_len, dims=in_dims)
        qlat_s, qpe_s = _q_proj(cq, cos, sin, w, bm=dec_batch, dims=q_dims, scale=scale)
        o_lat, oa_p, late = _attention(
            jnp.transpose(qlat_s, (1, 0, 2)), jnp.transpose(qpe_s, (1, 0, 2)),
            ckv_s[:, None, :], kpe_s[:, None, :], cache_ckv[i], jnp.swapaxes(cache_kpe[i], 1, 2),
            page_table, qlat_p, qpe_p, ckvb, kpeb, w["w_uvT"], [w[k] for k in LATE_WEIGHTS],
            batch=batch, seq=seq, tq=128, tk=512, cpp=16, nbuf=6)
        w.update(zip(LATE_WEIGHTS, late))

        h1, hf = _out_proj(hp, oa_p, ocm_p, w, bm=512)
        f = _ffn(hf, w, bm=1024, bh=512)
        hp = _ple(h1, f, p_prompt[i].reshape(batch * seq, -1), w, gfin, bm=512,
                  final=i == depth - 1)
        outs[0].append(ckv_p.reshape(batch, seq, kv_rank))
        outs[1].append(kpe_p.reshape(batch, seq, rope_dim))
        outs[2].append(vn_p.reshape(batch, seq, cm_width)[:, seq - chunk:])

        oa_s = _uv_proj(o_lat.reshape(dec_batch, heads * kv_rank), w["w_uvT"])
        h1, hf = _out_proj(hs, oa_s, ocm_s, w, bm=dec_batch)
        f = _ffn(hf, w, bm=dec_batch, bh=512)
        hs = _ple(h1, f, p_sample[i].reshape(dec_batch * dec_seq, -1), w, gfin, bm=dec_batch,
                  final=i == depth - 1)
        outs[3].append(ckv_s.reshape(dec_batch, dec_seq, kv_rank))
        outs[4].append(kpe_s.reshape(dec_batch, dec_seq, rope_dim))
        outs[5].append(vn_s.reshape(dec_batch, dec_seq, cm_width))

    y_prompt = hp.reshape(batch, seq, d_model)
    y_sample = hs.reshape(dec_batch, dec_seq, d_model)
    return (y_prompt, y_sample) + tuple(jnp.stack(o) for o in outs)
```

```python
import functools
import math

import jax
import jax.numpy as jnp
from jax import lax
from jax.experimental import pallas as pl
from jax.experimental.pallas import tpu as pltpu

F32 = jnp.float32
BF16 = jnp.bfloat16

EPS = 1e-6
NEG = -1e30
ROPE_THETA = 10000.0
LANES = 128
SUBLANES = 8
BF16_SUBLANES = 16
VMEM_LIMIT = 56 * 1024 * 1024


def _params(*sem):
    return pltpu.CompilerParams(dimension_semantics=sem, vmem_limit_bytes=VMEM_LIMIT)


def _rms(x, g):
    return x * lax.rsqrt(jnp.mean(x * x, axis=-1, keepdims=True) + EPS) * g


def _dot(a, b):
    return jnp.dot(a, b, preferred_element_type=F32)


def _dot_nt(a, b):
    return lax.dot_general(a, b, (((1,), (1,)), ((), ())), preferred_element_type=F32)


def _const_spec(shape):
    nd = len(shape)
    return pl.BlockSpec(shape, lambda *_: (0,) * nd)


def _in_proj_kernel(x_ref, gmix_ref, wa_ref, wpe_ref, wcm_ref, gq_ref, gkv_ref, gv_ref,
                    inv_ref, sign_ref, ws_ref, bs_ref,
                    cq_ref, ckv_ref, kpe_ref, ckvb_ref, kpeb_ref, ocm_ref, vn_ref,
                    cos_ref, sin_ref, *, bm, seq_len, pos0, q_rank, rope_dim, cm_width,
                    chunk, groups):
    i = pl.program_id(0)
    hn = _rms(x_ref[...], gmix_ref[...]).astype(BF16)

    za = _dot(hn, wa_ref[...])
    cq_ref[...] = _rms(za[:, :q_rank], gq_ref[...]).astype(BF16)
    ckv = _rms(za[:, q_rank:], gkv_ref[...])
    ckv_ref[...] = ckv
    ckvb_ref[...] = ckv.astype(BF16)

    row = i * bm + lax.broadcasted_iota(jnp.int32, (bm, LANES), 0)
    pos = pos0 + lax.rem(row, seq_len)
    ang = pos.astype(F32) * inv_ref[...]
    c = jnp.cos(ang)
    s = jnp.sin(ang) * sign_ref[...]
    cos_ref[...] = c
    sin_ref[...] = s
    zpe = _dot(hn, wpe_ref[...])
    kpe = (zpe * c + pltpu.roll(zpe, rope_dim // 2, axis=1) * s)[:, :rope_dim]
    kpe_ref[...] = kpe
    kpeb_ref[...] = kpe.astype(BF16)

    gz = jax.nn.gelu(_dot(hn, wcm_ref[...]))
    u = gz[:, :cm_width]
    vn = _rms(gz[:, cm_width:], gv_ref[...])
    vn_ref[...] = vn
    if seq_len % chunk == 0:
        vb = vn.astype(BF16)
        r = lax.broadcasted_iota(jnp.int32, (chunk, chunk), 0)
        cc = lax.broadcasted_iota(jnp.int32, (chunk, chunk), 1)
        head = cm_width // groups
        for g in range(groups):
            wsg = jnp.where(r >= cc, ws_ref[g], 0.0).astype(BF16)
            bcol = bs_ref[:, g:g + 1]
            for k in range(bm // chunk):
                rows = slice(k * chunk, (k + 1) * chunk)
                cols = slice(g * head, (g + 1) * head)
                mixed = _dot(wsg, vb[rows, cols]) + bcol
                ocm_ref[rows, cols] = (u[rows, cols] * mixed).astype(BF16)
    else:
        ocm_ref[...] = (u * (vn * ws_ref[...] + bs_ref[...])).astype(BF16)


def _in_proj(x, w, *, bm, seq_len, pos0, dims):
    n, d = x.shape
    q_rank, kv_rank, rope_dim, cm_width, chunk, groups = dims
    chunked = seq_len % chunk == 0
    assert chunked or seq_len == 1
    if chunked:
        ws, bs = w["w_s"], w["b_sT"]
    else:
        ws, bs = w["w_s_diag"], w["b_s_diag"]
    kern = functools.partial(_in_proj_kernel, bm=bm, seq_len=seq_len, pos0=pos0, q_rank=q_rank,
                             rope_dim=rope_dim, cm_width=cm_width, chunk=chunk, groups=groups)
    row = lambda width: pl.BlockSpec((bm, width), lambda i: (i, 0))
    consts = [w["g_mix"], w["w_a"], w["w_pe"], w["w_cm"], w["g_q"], w["g_kv"], w["g_v"],
              w["inv128"], w["sign128"], ws, bs]
    out_shape = (
        jax.ShapeDtypeStruct((n, q_rank), BF16),
        jax.ShapeDtypeStruct((n, kv_rank), F32),
        jax.ShapeDtypeStruct((n, rope_dim), F32),
        jax.ShapeDtypeStruct((n, kv_rank), BF16),
        jax.ShapeDtypeStruct((n, rope_dim), BF16),
        jax.ShapeDtypeStruct((n, cm_width), BF16),
        jax.ShapeDtypeStruct((n, cm_width), F32),
        jax.ShapeDtypeStruct((n, LANES), F32),
        jax.ShapeDtypeStruct((n, LANES), F32),
    )
    return pl.pallas_call(
        kern,
        grid=(n // bm,),
        in_specs=[row(d)] + [_const_spec(a.shape) for a in consts],
        out_specs=tuple(row(s.shape[1]) for s in out_shape),
        out_shape=out_shape,
        compiler_params=_params("parallel"),
        name="in_proj",
    )(x, *consts)


def _q_proj_kernel(cq_ref, wuq_ref, wuk_ref, cos_ref, sin_ref, qlat_ref, qpe_ref, *,
                   heads, nope, rope_dim, scale):
    q = _dot(cq_ref[...], wuq_ref[...])
    qn = q[:, :heads * nope].astype(BF16)
    for h in range(heads):
        qlat_ref[h] = (_dot(qn[:, h * nope:(h + 1) * nope], wuk_ref[h]) * scale).astype(BF16)
    qp = q[:, heads * nope:]
    width = heads * rope_dim
    reps = width // LANES
    c = jnp.concatenate([cos_ref[...]] * reps, axis=1)
    s = jnp.concatenate([sin_ref[...]] * reps, axis=1)
    half = rope_dim // 2
    lane = lax.broadcasted_iota(jnp.int32, qp.shape, 1)
    first_half = lax.rem(lane, rope_dim) < half
    partner = jnp.where(first_half, pltpu.roll(qp, width - half, axis=1),
                        pltpu.roll(qp, half, axis=1))
    qr = (qp * c + partner * s) * scale
    for h in range(heads):
        qpe_ref[h] = qr[:, h * rope_dim:(h + 1) * rope_dim].astype(BF16)


def _q_proj(cq, cos, sin, w, *, bm, dims, scale):
    n, q_rank = cq.shape
    heads, nope, rope_dim, kv_rank = dims
    kern = functools.partial(_q_proj_kernel, heads=heads, nope=nope, rope_dim=rope_dim,
                             scale=scale)
    row = lambda width: pl.BlockSpec((bm, width), lambda i: (i, 0))
    return pl.pallas_call(
        kern,
        grid=(n // bm,),
        in_specs=[row(q_rank), _const_spec(w["w_uq"].shape), _const_spec(w["w_ukT"].shape),
                  row(LANES), row(LANES)],
        out_specs=(pl.BlockSpec((heads, bm, kv_rank), lambda i: (0, i, 0)),
                   pl.BlockSpec((heads, bm, rope_dim), lambda i: (0, i, 0))),
        out_shape=(jax.ShapeDtypeStruct((heads, n, kv_rank), BF16),
                   jax.ShapeDtypeStruct((heads, n, rope_dim), BF16)),
        compiler_params=_params("parallel"),
        name="q_proj",
    )(cq, w["w_uq"], w["w_ukT"], cos, sin)


def _prompt_unit(qi, qlat_ref, qpe_ref, kc_ref, kp_ref, wuv_ref, o_ref, m_ref, l_ref, acc_ref,
                 *, tq, tk):
    heads, _, kv_rank = qlat_ref.shape
    rope_dim = qpe_ref.shape[-1]
    vdim = wuv_ref.shape[-1]
    rows = heads * tq
    q = qlat_ref[...].reshape(rows, kv_rank)
    qp = qpe_ref[...].reshape(rows, rope_dim)
    m_ref[...] = jnp.full(m_ref.shape, NEG, F32)
    l_ref[...] = jnp.zeros(l_ref.shape, F32)
    acc_ref[...] = jnp.zeros(acc_ref.shape, F32)
    n_full = (qi * tq) // tk

    def step(kb, masked):
        k0 = pl.multiple_of(kb * tk, tk)
        kc = kc_ref[pl.ds(k0, tk), :]
        kp = kp_ref[pl.ds(k0, tk), :]
        s = _dot_nt(q, kc) + _dot_nt(qp, kp)
        if masked:
            qpos = qi * tq + lax.rem(lax.broadcasted_iota(jnp.int32, (rows, tk), 0), tq)
            col = lax.broadcasted_iota(jnp.int32, (rows, tk), 1)
            s = jnp.where(k0 + col <= qpos, s, NEG)
        m_old = m_ref[...]
        m_new = jnp.maximum(m_old, jnp.max(s, axis=1, keepdims=True))
        corr = jnp.exp(m_old - m_new)
        p = jnp.exp(s - jnp.concatenate([m_new] * (tk // LANES), axis=1))
        psum = p[:, :LANES]
        for t in range(1, tk // LANES):
            psum = psum + p[:, t * LANES:(t + 1) * LANES]
        l_ref[...] = l_ref[...] * corr + psum
        acc_ref[...] = (acc_ref[...] * jnp.concatenate([corr] * (kv_rank // LANES), axis=1)
                        + _dot(p.astype(BF16), kc))
        m_ref[...] = m_new

    def body(kb, carry):
        step(kb, False)
        return carry

    lax.fori_loop(0, n_full, body, 0)
    step(n_full, True)
    o = (acc_ref[...] / jnp.sum(l_ref[...], axis=1, keepdims=True)).astype(BF16)
    for h in range(heads):
        o_ref[:, h * vdim:(h + 1) * vdim] = _dot(o[h * tq:(h + 1) * tq], wuv_ref[h]).astype(BF16)


def _sample_seq(b, nb, pt_ref, qlat_ref, qpe_ref, cnew_ref, knew_ref, ckv_hbm, kpe_hbm, o_ref,
                kbuf, pbuf, sem_k, sem_p, *, cpp, nbuf, n_chunks):
    total = nb * n_chunks
    page, kv_rank = kbuf.shape[2], kbuf.shape[3]
    ahead = nbuf - 2

    def copies(g, slot):
        out = []
        for j in range(cpp):
            pid = pt_ref[g * cpp + j]
            out.append(pltpu.make_async_copy(ckv_hbm.at[pid], kbuf.at[slot, j], sem_k.at[slot, j]))
            out.append(pltpu.make_async_copy(kpe_hbm.at[pid], pbuf.at[slot, j], sem_p.at[slot, j]))
        return out

    @pl.when(b == 0)
    def _():
        for g in range(ahead):
            for cp in copies(g, g % nbuf):
                cp.start()

    q = qlat_ref[...]
    qp = qpe_ref[...]
    heads = q.shape[0]
    m = jnp.full((heads, 1), NEG, F32)
    l = jnp.zeros((heads, 1), F32)
    acc = jnp.zeros((heads, kv_rank), F32)

    def fetch(c):
        g = b * n_chunks + c
        slot = lax.rem(g, nbuf)
        for cp in copies(g, slot):
            cp.wait()
        kc = kbuf[slot].reshape(cpp * page, kv_rank).astype(BF16)
        kp = jnp.concatenate([pbuf[slot, j] for j in range(cpp)], axis=1).astype(BF16)
        s = _dot_nt(q, kc) + _dot(qp, kp)
        for cp in copies(jnp.minimum(g + ahead, total - 1), lax.rem(g + ahead, nbuf)):
            cp.start()
        return kc, s

    kc, s = fetch(0)
    for c in range(n_chunks):
        if c + 1 < n_chunks:
            kc_next, s_next = fetch(c + 1)
        m_new = jnp.maximum(m, jnp.max(s, axis=1, keepdims=True))
        corr = jnp.exp(m - m_new)
        p = jnp.exp(s - m_new)
        l = l * corr + jnp.sum(p, axis=1, keepdims=True)
        acc = acc * corr + _dot(p.astype(BF16), kc)
        m = m_new
        if c + 1 < n_chunks:
            kc, s = kc_next, s_next

    @pl.when(b == nb - 1)
    def _():
        for k in range(ahead):
            for cp in copies(total - 1, lax.rem(total + k, nbuf)):
                cp.wait()

    cn = cnew_ref[...]
    s_new = (jnp.sum(q.astype(F32) * cn, axis=1, keepdims=True)
             + jnp.sum(qp.astype(F32) * knew_ref[...], axis=1, keepdims=True))
    m_tot = jnp.maximum(m, s_new)
    c_old = jnp.exp(m - m_tot)
    p_new = jnp.exp(s_new - m_tot)
    o_ref[...] = (acc * c_old + p_new * cn) / (l * c_old + p_new)


def _attn_kernel(pt_ref, sq_ref, sqp_ref, cnew_ref, knew_ref, pq_ref, pqp_ref, kc_ref, kp_ref,
                 wuv_ref, *rest, n_cast, nq, split, tq, tk, cpp, nbuf, n_chunks):
    cast_in = rest[:n_cast]
    ckv_hbm, kpe_hbm, so_ref, po_ref = rest[n_cast:n_cast + 4]
    cast_out = rest[n_cast + 4:2 * n_cast + 4]
    kbuf, pbuf, sem_k, sem_p, m_ref, l_ref, acc_ref = rest[2 * n_cast + 4:]
    b = pl.program_id(0)
    for src, dst in zip(cast_in, cast_out):
        dst[...] = src[...].astype(BF16)
    _prompt_unit(lax.rem(b // split, nq), pq_ref, pqp_ref, kc_ref, kp_ref, wuv_ref, po_ref,
                 m_ref, l_ref, acc_ref, tq=tq, tk=tk)
    _sample_seq(b, pl.num_programs(0), pt_ref, sq_ref, sqp_ref, cnew_ref, knew_ref, ckv_hbm,
                kpe_hbm, so_ref, kbuf, pbuf, sem_k, sem_p, cpp=cpp, nbuf=nbuf, n_chunks=n_chunks)


def _slab_spec(arr, steps):
    r, c = arr.shape
    n_slabs = steps
    while r % n_slabs or (r // n_slabs) % BF16_SUBLANES:
        n_slabs //= 2
    assert n_slabs >= 1
    return pl.BlockSpec((r // n_slabs, c), lambda b, pt: (jnp.minimum(b, n_slabs - 1), 0))


def _attention(sq, sqp, cnew, knew, cache_ckv, cache_kpe_t, page_table, pq, pqp, kc, kp, wuv,
               to_cast, *, batch, seq, tq, tk, cpp, nbuf):
    bd, heads, kv_rank = sq.shape
    rope_dim = sqp.shape[-1]
    n = pq.shape[1]
    vdim = wuv.shape[-1]
    nq = seq // tq
    split = bd // (batch * nq)
    assert split * batch * nq == bd and heads % split == 0
    hpg = heads // split
    assert tk % tq == 0 and seq % tk == 0
    n_pages = page_table.shape[1]
    page = cache_ckv.shape[1]
    n_chunks = n_pages // cpp
    assert n_pages % cpp == 0 and nbuf >= 3
    pt = page_table.reshape(-1)
    kern = functools.partial(_attn_kernel, n_cast=len(to_cast), nq=nq, split=split, tq=tq, tk=tk,
                             cpp=cpp, nbuf=nbuf, n_chunks=n_chunks)
    per_b = lambda r, width: pl.BlockSpec((None, r, width), lambda b, pt_ref: (b, 0, 0))
    slabs = [_slab_spec(a, bd) for a in to_cast]
    any_spec = pl.BlockSpec(memory_space=pl.ANY)
    grid_spec = pltpu.PrefetchScalarGridSpec(
        num_scalar_prefetch=1,
        grid=(bd,),
        in_specs=[per_b(heads, kv_rank), per_b(heads, rope_dim), per_b(1, kv_rank),
                  per_b(1, rope_dim),
                  pl.BlockSpec((hpg, tq, kv_rank), lambda b, pt_ref: (b % split, b // split, 0)),
                  pl.BlockSpec((hpg, tq, rope_dim), lambda b, pt_ref: (b % split, b // split, 0)),
                  pl.BlockSpec((seq, kv_rank), lambda b, pt_ref: (b // (split * nq), 0)),
                  pl.BlockSpec((seq, rope_dim), lambda b, pt_ref: (b // (split * nq), 0)),
                  pl.BlockSpec((hpg,) + wuv.shape[1:], lambda b, pt_ref: (b % split, 0, 0))]
                 + slabs + [any_spec, any_spec],
        out_specs=[per_b(heads, kv_rank),
                   pl.BlockSpec((tq, hpg * vdim), lambda b, pt_ref: (b // split, b % split))]
                  + slabs,
        scratch_shapes=[pltpu.VMEM((nbuf, cpp, page, kv_rank), F32),
                        pltpu.VMEM((nbuf, cpp, rope_dim, page), F32),
                        pltpu.SemaphoreType.DMA((nbuf, cpp)), pltpu.SemaphoreType.DMA((nbuf, cpp)),
                        pltpu.VMEM((hpg * tq, LANES), F32), pltpu.VMEM((hpg * tq, LANES), F32),
                        pltpu.VMEM((hpg * tq, kv_rank), F32)],
    )
    outs = pl.pallas_call(
        kern,
        grid_spec=grid_spec,
        out_shape=[jax.ShapeDtypeStruct((bd, heads, kv_rank), F32),
                   jax.ShapeDtypeStruct((n, heads * vdim), BF16)]
                  + [jax.ShapeDtypeStruct(a.shape, BF16) for a in to_cast],
        compiler_params=_params("arbitrary"),
        name="attention",
    )(pt, sq, sqp, cnew, knew, pq, pqp, kc, kp, wuv, *to_cast, cache_ckv, cache_kpe_t)
    return outs[0], outs[1], outs[2:]


def _uv_kernel(o_ref, wuv_ref, out_ref, *, heads):
    kv_rank = wuv_ref.shape[1]
    vdim = wuv_ref.shape[2]
    for h in range(heads):
        oh = o_ref[:, h * kv_rank:(h + 1) * kv_rank].astype(BF16)
        out_ref[:, h * vdim:(h + 1) * vdim] = _dot(oh, wuv_ref[h]).astype(BF16)


def _uv_proj(o_lat, wuv):
    n = o_lat.shape[0]
    heads, kv_rank, vdim = wuv.shape
    return pl.pallas_call(
        functools.partial(_uv_kernel, heads=heads),
        grid=(1,),
        in_specs=[_const_spec(o_lat.shape), _const_spec(wuv.shape)],
        out_specs=_const_spec((n, heads * vdim)),
        out_shape=jax.ShapeDtypeStruct((n, heads * vdim), BF16),
        compiler_params=_params("arbitrary"),
        name="uv_proj",
    )(o_lat, wuv)


def _out_proj_kernel(x_ref, oa_ref, ocm_ref, wo_ref, gffn_ref, h_ref, hf_ref):
    wa = oa_ref.shape[1]
    h1 = x_ref[...] + _dot(oa_ref[...], wo_ref[:wa, :]) + _dot(ocm_ref[...], wo_ref[wa:, :])
    h_ref[...] = h1
    hf_ref[...] = _rms(h1, gffn_ref[...]).astype(BF16)


def _out_proj(x, oa, ocm, w, *, bm):
    n, d = x.shape
    row = lambda width: pl.BlockSpec((bm, width), lambda i: (i, 0))
    return pl.pallas_call(
        _out_proj_kernel,
        grid=(n // bm,),
        in_specs=[row(d), row(oa.shape[1]), row(ocm.shape[1]), _const_spec(w["w_o"].shape),
                  _const_spec(w["g_ffn"].shape)],
        out_specs=(row(d), row(d)),
        out_shape=(jax.ShapeDtypeStruct((n, d), F32), jax.ShapeDtypeStruct((n, d), BF16)),
        compiler_params=_params("parallel"),
        name="out_proj",
    )(x, oa, ocm, w["w_o"], w["g_ffn"])


def _ffn_kernel(hf_ref, wg_ref, wu_ref, wd_ref, out_ref):
    j = pl.program_id(1)

    @pl.when(j == 0)
    def _():
        out_ref[...] = jnp.zeros(out_ref.shape, F32)

    hf = hf_ref[...]
    t = (jax.nn.silu(_dot(hf, wg_ref[...])) * _dot(hf, wu_ref[...])).astype(BF16)
    out_ref[...] += _dot(t, wd_ref[...])


def _ffn(hf, w, *, bm, bh):
    n, d = hf.shape
    hidden = w["w_gate"].shape[1]
    assert hidden % bh == 0 and n % bm == 0
    return pl.pallas_call(
        _ffn_kernel,
        grid=(n // bm, hidden // bh),
        in_specs=[pl.BlockSpec((bm, d), lambda i, j: (i, 0)),
                  pl.BlockSpec((d, bh), lambda i, j: (0, j)),
                  pl.BlockSpec((d, bh), lambda i, j: (0, j)),
                  pl.BlockSpec((bh, d), lambda i, j: (j, 0))],
        out_specs=pl.BlockSpec((bm, d), lambda i, j: (i, 0)),
        out_shape=jax.ShapeDtypeStruct((n, d), F32),
        compiler_params=_params("parallel", "arbitrary"),
        name="ffn",
    )(hf, w["w_gate"], w["w_up"], w["w_down"])


def _ple_kernel(h_ref, f_ref, p_ref, gple_ref, wg_ref, wp_ref, gfin_ref, y_ref, *, final):
    h = h_ref[...] + f_ref[...]
    gate = jax.nn.sigmoid(_dot(_rms(h, gple_ref[...]).astype(BF16), wg_ref[...]))
    h3 = h + gate * _dot(p_ref[...].astype(BF16), wp_ref[...])
    y_ref[...] = _rms(h3, gfin_ref[...]) if final else h3


def _ple(h, f, p, w, g_final, *, bm, final):
    n, d = h.shape
    row = lambda width: pl.BlockSpec((bm, width), lambda i: (i, 0))
    return pl.pallas_call(
        functools.partial(_ple_kernel, final=final),
        grid=(n // bm,),
        in_specs=[row(d), row(d), row(p.shape[1]), _const_spec(w["g_ple"].shape),
                  _const_spec(w["w_ple_gate"].shape), _const_spec(w["w_ple_proj"].shape),
                  _const_spec(g_final.shape)],
        out_specs=row(d),
        out_shape=jax.ShapeDtypeStruct((n, d), F32),
        compiler_params=_params("parallel"),
        name="ple",
    )(h, f, p, w["g_ple"], w["w_ple_gate"], w["w_ple_proj"], g_final)


def _prep_weights(i, g_mix, w_in, g_q, w_uq, g_kv, w_uk, w_uv, g_v, w_s, b_s, w_o, g_ffn,
                  w_ffn_gate, w_ffn_up, w_ffn_down, g_ple, w_ple_gate, w_ple_proj):
    q_rank = g_q.shape[1]
    kv_rank = g_kv.shape[1]
    heads, qk_dim = w_uq.shape[2], w_uq.shape[3]
    nope = w_uk.shape[3]
    rope_dim = qk_dim - nope
    groups = w_s.shape[1]
    cm_width = g_v.shape[1]
    wi = w_in[i]
    pe = wi[:, q_rank + kv_rank:q_rank + kv_rank + rope_dim]
    half = rope_dim // 2
    inv = ROPE_THETA ** (-jnp.arange(half, dtype=F32) / half)
    lane = jnp.arange(LANES)
    return {
        "g_mix": g_mix[i][None], "g_q": g_q[i][None], "g_kv": g_kv[i][None], "g_v": g_v[i][None],
        "g_ffn": g_ffn[i][None], "g_ple": g_ple[i][None],
        "w_a": wi[:, :q_rank + kv_rank].astype(BF16),
        "w_pe": jnp.concatenate([pe] * (LANES // rope_dim), axis=1).astype(BF16),
        "w_cm": wi[:, q_rank + kv_rank + rope_dim:].astype(BF16),
        "inv128": jnp.tile(inv, LANES // half)[None],
        "sign128": jnp.where((lane % rope_dim) < half, -1.0, 1.0).astype(F32)[None],
        "w_s": w_s[i], "b_sT": b_s[i].T,
        "w_s_diag": jnp.repeat(w_s[i][:, 0, 0], cm_width // groups)[None],
        "b_s_diag": jnp.repeat(b_s[i][:, 0], cm_width // groups)[None],
        "w_uq": jnp.concatenate([w_uq[i][:, :, :nope].reshape(q_rank, heads * nope),
                                 w_uq[i][:, :, nope:].reshape(q_rank, heads * rope_dim)],
                                axis=1).astype(BF16),
        "w_ukT": jnp.transpose(w_uk[i], (1, 2, 0)).astype(BF16),
        "w_uvT": jnp.transpose(w_uv[i], (1, 0, 2)).astype(BF16),
        "w_ple_proj": w_ple_proj[i].astype(BF16),
        "w_o": w_o[i], "w_gate": w_ffn_gate[i], "w_up": w_ffn_up[i], "w_down": w_ffn_down[i],
        "w_ple_gate": w_ple_gate[i],
    }


LATE_WEIGHTS = ("w_o", "w_gate", "w_up", "w_down", "w_ple_gate")


def kernel(x_prompt, x_sample, p_prompt, p_sample, cache_ckv, cache_kpe, page_table, g_mix, w_in,
           g_q, w_uq, g_kv, w_uk, w_uv, g_v, w_s, b_s, w_o, g_ffn, w_ffn_gate, w_ffn_up,
           w_ffn_down, g_ple, w_ple_gate, w_ple_proj, g_final):
    batch, seq, d_model = x_prompt.shape
    dec_batch, dec_seq, _ = x_sample.shape
    depth = g_mix.shape[0]
    q_rank, kv_rank = g_q.shape[1], g_kv.shape[1]
    heads, qk_dim = w_uq.shape[2], w_uq.shape[3]
    nope = w_uk.shape[3]
    rope_dim = qk_dim - nope
    cm_width = g_v.shape[1]
    groups, chunk = w_s.shape[1], w_s.shape[2]
    page = cache_ckv.shape[2]
    past_len = page_table.shape[1] * page
    scale = 1.0 / math.sqrt(nope + rope_dim)
    assert dec_seq == 1 and seq % chunk == 0
    in_dims = (q_rank, kv_rank, rope_dim, cm_width, chunk, groups)
    q_dims = (heads, nope, rope_dim, kv_rank)
    gfin = g_final[None]

    hp = x_prompt.reshape(batch * seq, d_model)
    hs = x_sample.reshape(dec_batch * dec_seq, d_model)
    outs = [[] for _ in range(6)]
    for i in range(depth):
        w = _prep_weights(i, g_mix, w_in, g_q, w_uq, g_kv, w_uk, w_uv, g_v, w_s, b_s, w_o, g_ffn,
                          w_ffn_gate, w_ffn_up, w_ffn_down, g_ple, w_ple_gate, w_ple_proj)

        cq, ckv_p, kpe_p, ckvb, kpeb, ocm_p, vn_p, cos, sin = _in_proj(
            hp, w, bm=512, seq_len=seq, pos0=0, dims=in_dims)
        qlat_p, qpe_p = _q_proj(cq, cos, sin, w, bm=512, dims=q_dims, scale=scale)
        cq, ckv_s, kpe_s, _, _, ocm_s, vn_s, cos, sin = _in_proj(
            hs, w, bm=dec_batch, seq_len=dec_seq, pos0=past_len, dims=in_dims)
        qlat_s, qpe_s = _q_proj(cq, cos, sin, w, bm=dec_batch, dims=q_dims, scale=scale)
        o_lat, oa_p, late = _attention(
            jnp.transpose(qlat_s, (1, 0, 2)), jnp.transpose(qpe_s, (1, 0, 2)),
            ckv_s[:, None, :], kpe_s[:, None, :], cache_ckv[i], jnp.swapaxes(cache_kpe[i], 1, 2),
            page_table, qlat_p, qpe_p, ckvb, kpeb, w["w_uvT"], [w[k] for k in LATE_WEIGHTS],
            batch=batch, seq=seq, tq=128, tk=512, cpp=16, nbuf=6)
        w.update(zip(LATE_WEIGHTS, late))

        h1, hf = _out_proj(hp, oa_p, ocm_p, w, bm=512)
        f = _ffn(hf, w, bm=1024, bh=512)
        hp = _ple(h1, f, p_prompt[i].reshape(batch * seq, -1), w, gfin, bm=512,
                  final=i == depth - 1)
        outs[0].append(ckv_p.reshape(batch, seq, kv_rank))
        outs[1].append(kpe_p.reshape(batch, seq, rope_dim))
        outs[2].append(vn_p.reshape(batch, seq, cm_width)[:, seq - chunk:])

        oa_s = _uv_proj(o_lat.reshape(dec_batch, heads * kv_rank), w["w_uvT"])
        h1, hf = _out_proj(hs, oa_s, ocm_s, w, bm=dec_batch)
        f = _ffn(hf, w, bm=dec_batch, bh=512)
        hs = _ple(h1, f, p_sample[i].reshape(dec_batch * dec_seq, -1), w, gfin, bm=dec_batch,
                  final=i == depth - 1)
        outs[3].append(ckv_s.reshape(dec_batch, dec_seq, kv_rank))
        outs[4].append(kpe_s.reshape(dec_batch, dec_seq, rope_dim))
        outs[5].append(vn_s.reshape(dec_batch, dec_seq, cm_width))

    y_prompt = hp.reshape(batch, seq, d_model)
    y_sample = hs.reshape(dec_batch, dec_seq, d_model)
    return (y_prompt, y_sample) + tuple(jnp.stack(o) for o in outs)
```
